```python
import jax, jax.numpy as jnp
from jax import lax
import numpy as np

D_MODEL = 1024
BATCH = 16
SEQ = 4096
DEPTH = 2

HEAD_DIM = 64
N_HEADS = D_MODEL // HEAD_DIM
DILATED_GROUPS = ((128, 1), (512, 4), (2048, 16))
N_GROUPS = len(DILATED_GROUPS)
BAND_BLOCK = 128
ROT_DIM = HEAD_DIM // 4
ROPE_THETA = 500000.0
FOX_BLOCK = 128
D_FF = 2816
N_A_LAYERS = DEPTH // 2
N_B_LAYERS = DEPTH - N_A_LAYERS
HD = N_HEADS * HEAD_DIM
EPS = 1e-6

kernel_name = "yoco_dilated_fox_macaron_trunk"


def rms_norm(x, g):
    xf = x.astype(jnp.float32)
    y = xf * lax.rsqrt(jnp.mean(xf * xf, axis=-1, keepdims=True) + EPS)
    return (y * g.astype(jnp.float32)).astype(x.dtype)


def swiglu(x, w_in, w_out):
    gate, up = jnp.split(x @ w_in, 2, axis=-1)
    return (jax.nn.silu(gate) * up) @ w_out


def rope_partial(x, positions):
    half = ROT_DIM // 2
    inv_freq = ROPE_THETA ** (-jnp.arange(0, ROT_DIM, 2, dtype=jnp.float32) / ROT_DIM)
    ang = positions.astype(jnp.float32)[..., None] * inv_freq
    cos, sin = jnp.cos(ang)[:, :, None, :], jnp.sin(ang)[:, :, None, :]
    xr = x[..., :ROT_DIM].astype(jnp.float32)
    x1, x2 = xr[..., :half], xr[..., half:]
    rot = jnp.concatenate([x1 * cos - x2 * sin, x2 * cos + x1 * sin], axis=-1)
    return jnp.concatenate([rot.astype(x.dtype), x[..., ROT_DIM:]], axis=-1)


def dilated_band_attention(q, k, v, window, dilation):
    b, s, h, dh = q.shape
    n_steps = window // dilation
    span = dilation * BAND_BLOCK
    s_pad = -(-s // span) * span
    seq_len = s_pad // dilation
    nb = seq_len // BAND_BLOCK
    pad = ((0, 0), (0, s_pad - s), (0, 0), (0, 0))

    def to_blocks(t):
        t = jnp.pad(t, pad).reshape(b, seq_len, dilation, h, dh).transpose(0, 2, 1, 3, 4)
        return t.reshape(b, dilation, nb, BAND_BLOCK, h, dh)

    def with_prev(t):
        prev = jnp.pad(t, ((0, 0), (0, 0), (1, 0), (0, 0), (0, 0), (0, 0)))[:, :, :-1]
        return jnp.concatenate([prev, t], axis=3)

    qb = to_blocks(q)
    kk = with_prev(to_blocks(k))
    vv = with_prev(to_blocks(v))
    scores = jnp.einsum('brnqhd,brnkhd->brnhqk', qb, kk).astype(jnp.float32) * (dh ** -0.5)
    qi = jnp.arange(BAND_BLOCK)[:, None]
    kj = jnp.arange(2 * BAND_BLOCK)[None, :]
    dist = qi + BAND_BLOCK - kj
    blk = jnp.arange(nb)[:, None, None]
    valid = (dist >= 0) & (dist <= n_steps) & ((blk > 0) | (kj >= BAND_BLOCK))
    scores = jnp.where(valid[:, None], scores, -jnp.inf)
    lse = jax.nn.logsumexp(scores, axis=-1)
    probs = jnp.exp(scores - lse[..., None])
    out = jnp.einsum('brnhqk,brnkhd->brnqhd', probs.astype(v.dtype), vv)
    out = out.transpose(0, 2, 3, 1, 4, 5).reshape(b, s_pad, h, dh)[:, :s]
    lse = lse.transpose(0, 2, 4, 1, 3).reshape(b, s_pad, h)[:, :s]
    return out, lse


def dilated_mixture_mixer(hn, positions, w_qkv, q_norm, k_norm, w_o):
    b, s, _ = hn.shape
    qkv = (hn @ w_qkv).reshape(b, s, N_GROUPS, 3, N_HEADS, HEAD_DIM)
    outs, lses = [], []
    for g, (window, dilation) in enumerate(DILATED_GROUPS):
        q = rope_partial(rms_norm(qkv[:, :, g, 0], q_norm[g]), positions)
        k = rope_partial(rms_norm(qkv[:, :, g, 1], k_norm[g]), positions)
        o, lse = dilated_band_attention(q, k, qkv[:, :, g, 2], window, dilation)
        outs.append(o.astype(jnp.float32))
        lses.append(lse)
    alpha = jax.nn.softmax(jnp.stack(lses, axis=0), axis=0)
    mixed = jnp.sum(alpha[..., None] * jnp.stack(outs, axis=0), axis=0).astype(hn.dtype)
    return mixed.reshape(b, s, HD) @ w_o


def shared_kv(hn, w_kv, b_f, k_norm):
    b, s, _ = hn.shape
    proj = hn @ w_kv
    k = rms_norm(proj[..., :HD].reshape(b, s, N_HEADS, HEAD_DIM), k_norm)
    v = proj[..., HD:2 * HD].reshape(b, s, N_HEADS, HEAD_DIM)
    log_f = jax.nn.log_sigmoid(proj[..., 2 * HD:].astype(jnp.float32) + b_f.astype(jnp.float32))
    cum = jnp.cumsum(log_f, axis=1)
    return k, v, cum


def forgetting_attention(hn, k, v, cum, w_q, q_norm, w_o):
    b, s, _ = hn.shape
    q = rms_norm((hn @ w_q).reshape(b, s, N_HEADS, HEAD_DIM), q_norm)
    nb = s // FOX_BLOCK
    q_blocks = q.reshape(b, nb, FOX_BLOCK, N_HEADS, HEAD_DIM).transpose(1, 0, 2, 3, 4)
    c_blocks = cum.reshape(b, nb, FOX_BLOCK, N_HEADS).transpose(1, 0, 2, 3)
    ck = cum.transpose(0, 2, 1)[:, :, None, :]
    key_pos = jnp.arange(s)
    scale = HEAD_DIM ** -0.5

    def block(args):
        qb, cb, bi = args
        logits = jnp.einsum('bqhd,bshd->bhqs', qb, k).astype(jnp.float32) * scale
        logits = logits + (cb.transpose(0, 2, 1)[..., None] - ck)
        qpos = bi * FOX_BLOCK + jnp.arange(FOX_BLOCK)
        logits = jnp.where(key_pos[None, :] <= qpos[:, None], logits, -jnp.inf)
        p = jax.nn.softmax(logits, axis=-1)
        return jnp.einsum('bhqs,bshd->bqhd', p.astype(v.dtype), v)

    o = lax.map(block, (q_blocks, c_blocks, jnp.arange(nb)))
    o = o.transpose(1, 0, 2, 3, 4).reshape(b, s, HD)
    return o @ w_o


def setup_inputs(seed: int = 0) -> dict:
    key = jax.random.key(seed)
    ks = jax.random.split(key, 20)
    f32 = jnp.float32

    def nrm(k, shape, fan_in):
        return jax.random.normal(k, shape, f32) * (fan_in ** -0.5)

    def gain(k, shape):
        return 1.0 + 0.05 * jax.random.normal(k, shape, f32)

    x = jax.random.normal(ks[0], (BATCH, SEQ, D_MODEL), f32)
    offset = jax.random.randint(ks[1], (BATCH, 1), 0, 1024, dtype=jnp.int32)
    positions = (jnp.arange(SEQ, dtype=jnp.int32)[None, :] + offset).astype(jnp.int32)
    return {
        "x": x,
        "positions": positions,
        "ffn_norm": gain(ks[2], (DEPTH, 2, D_MODEL)),
        "ffn_w_in": nrm(ks[3], (DEPTH, 2, D_MODEL, 2 * D_FF), D_MODEL),
        "ffn_w_out": nrm(ks[4], (DEPTH, 2, D_FF, D_MODEL), D_FF),
        "mix_norm": gain(ks[5], (DEPTH, D_MODEL)),
        "a_w_qkv": nrm(ks[6], (N_A_LAYERS, D_MODEL, N_GROUPS * 3 * HD), D_MODEL),
        "a_q_norm": gain(ks[7], (N_A_LAYERS, N_GROUPS, HEAD_DIM)),
        "a_k_norm": gain(ks[8], (N_A_LAYERS, N_GROUPS, HEAD_DIM)),
        "a_w_o": nrm(ks[9], (N_A_LAYERS, HD, D_MODEL), HD),
        "kv_norm": gain(ks[10], (D_MODEL,)),
        "kv_w": nrm(ks[11], (D_MODEL, 2 * HD + N_HEADS), D_MODEL),
        "kv_b_f": 0.1 * jax.random.normal(ks[12], (N_HEADS,), f32),
        "kv_k_norm": gain(ks[13], (HEAD_DIM,)),
        "b_w_q": nrm(ks[14], (N_B_LAYERS, D_MODEL, HD), D_MODEL),
        "b_q_norm": gain(ks[15], (N_B_LAYERS, HEAD_DIM)),
        "b_w_o": nrm(ks[16], (N_B_LAYERS, HD, D_MODEL), HD),
    }


def reference(x, positions, ffn_norm, ffn_w_in, ffn_w_out, mix_norm, a_w_qkv, a_q_norm, a_k_norm, a_w_o,
              kv_norm, kv_w, kv_b_f, kv_k_norm, b_w_q, b_q_norm, b_w_o):
    h = x
    k_sh = v_sh = cum_sh = None
    for layer in range(DEPTH):
        if layer == N_A_LAYERS:
            k_sh, v_sh, cum_sh = shared_kv(rms_norm(h, kv_norm), kv_w, kv_b_f, kv_k_norm)
        h = h + 0.5 * swiglu(rms_norm(h, ffn_norm[layer, 0]), ffn_w_in[layer, 0], ffn_w_out[layer, 0])
        hn = rms_norm(h, mix_norm[layer])
        if layer < N_A_LAYERS:
            h = h + dilated_mixture_mixer(hn, positions, a_w_qkv[layer], a_q_norm[layer],
                                          a_k_norm[layer], a_w_o[layer])
        else:
            j = layer - N_A_LAYERS
            h = h + forgetting_attention(hn, k_sh, v_sh, cum_sh, b_w_q[j], b_q_norm[j], b_w_o[j])
        h = h + 0.5 * swiglu(rms_norm(h, ffn_norm[layer, 1]), ffn_w_in[layer, 1], ffn_w_out[layer, 1])
    return h
```

```python
import functools

import numpy as np
import jax
import jax.numpy as jnp
from jax import lax
from jax.experimental import pallas as pl
from jax.experimental.pallas import tpu as pltpu

F32 = jnp.float32
BF16 = jnp.bfloat16

HEAD_DIM = 64
DILATED_GROUPS = ((128, 1), (512, 4), (2048, 16))
BAND = 128
ROT_DIM = HEAD_DIM // 4
ROPE_THETA = 500000.0
EPS = 1e-6
LANES = 128
MXU_DIM = 256
NEG = -1e30
MIB = 1024 * 1024


def _params(semantics, vmem_mib):
    return pltpu.CompilerParams(dimension_semantics=semantics, vmem_limit_bytes=vmem_mib * MIB)


def _resident(shape):
    nd = len(shape)
    return pl.BlockSpec(shape, lambda *_: (0,) * nd, pipeline_mode=pl.Buffered(1))


def _dot(a, b):
    return jnp.dot(a, b, preferred_element_type=F32)


def _dot_nt(a, b):
    return lax.dot_general(a, b, (((1,), (1,)), ((), ())), preferred_element_type=F32)


def _rms(x, g):
    ms = jnp.mean(x * x, axis=-1, keepdims=True)
    return x * lax.rsqrt(ms + EPS) * g


def _head_rms(y, bd_ref, gain):
    outs = []
    for c in range(y.shape[1] // MXU_DIM):
        yc = y[:, c * MXU_DIM:(c + 1) * MXU_DIM]
        ms = _dot((yc * yc).astype(BF16), bd_ref[...])
        outs.append(yc * lax.rsqrt(ms + EPS))
    return jnp.concatenate(outs, axis=1) * gain


def _rope_tables(pos_col, f_row):
    ang = pos_col.astype(F32) * f_row
    j = lax.broadcasted_iota(jnp.int32, (1, LANES), 1) % HEAD_DIM
    c, s = jnp.cos(ang), jnp.sin(ang)
    half = ROT_DIM // 2
    s_up = jnp.where(j < half, -s, 0.0)
    s_dn = jnp.where((j >= half) & (j < ROT_DIM), s, 0.0)
    return c, s_up, s_dn


def _rope(y, tabs):
    c, s_up, s_dn = tabs
    half = ROT_DIM // 2
    outs = []
    for k in range(y.shape[1] // LANES):
        yc = y[:, k * LANES:(k + 1) * LANES]
        outs.append(yc * c + pltpu.roll(yc, LANES - half, 1) * s_up + pltpu.roll(yc, half, 1) * s_dn)
    return jnp.concatenate(outs, axis=1)


def _split_bf16(v, parts):
    out, r = [], v
    for _ in range(parts):
        p = r.astype(BF16)
        out.append(p)
        r = r - p.astype(F32)
    return out


def _ffn_kernel(x_ref, g_ref, win_ref, wout_ref, o_ref, *, d_ff, chunk):
    x = x_ref[...]
    xn = _rms(x, g_ref[...]).astype(BF16)
    acc = jnp.zeros_like(x)
    for c in range(d_ff // chunk):
        gate = _dot(xn, win_ref[:, c * chunk:(c + 1) * chunk])
        up = _dot(xn, win_ref[:, d_ff + c * chunk:d_ff + (c + 1) * chunk])
        mid = (gate * jax.nn.sigmoid(gate) * up).astype(BF16)
        acc = acc + _dot(mid, wout_ref[c * chunk:(c + 1) * chunk, :])
    o_ref[...] = x + 0.5 * acc


def _ffn(h2, g, w_in, w_out, tm=512):
    n, d = h2.shape
    d_ff = w_out.shape[0]
    chunk = d_ff // 2 if (d_ff // 2) % LANES == 0 else d_ff
    return pl.pallas_call(
        functools.partial(_ffn_kernel, d_ff=d_ff, chunk=chunk),
        grid=(n // tm,),
        in_specs=[pl.BlockSpec((tm, d), lambda i: (i, 0)), _resident((1, d)),
                  _resident(w_in.shape), _resident(w_out.shape)],
        out_specs=pl.BlockSpec((tm, d), lambda i: (i, 0)),
        out_shape=jax.ShapeDtypeStruct((n, d), F32),
        compiler_params=_params(("parallel",), 52),
        name="ffn",
    )(h2, g.reshape(1, d), w_in, w_out)


def _perm_matrix(tm, dil):
    p = np.arange(tm)
    t = (p % (tm // dil)) * dil + p // (tm // dil)
    m = np.zeros((tm, tm), np.float32)
    m[p, t] = 1.0
    return m


def _a_qkv_kernel(x_ref, g_ref, w_ref, p4_ref, p16_ref, pos1_ref, pos4_ref, pos16_ref, f_ref, gq_ref, gk_ref,
                  bd_ref, o1_ref, o4_ref, o16_ref):
    hd = gq_ref.shape[-1]
    xn = _rms(x_ref[0], g_ref[...]).astype(BF16)
    per_group = ((None, pos1_ref, o1_ref), (p4_ref, pos4_ref, o4_ref), (p16_ref, pos16_ref, o16_ref))
    for g, (p_ref, pos_ref, o_ref) in enumerate(per_group):
        xp = xn if p_ref is None else _dot(p_ref[...], xn).astype(BF16)
        tabs = _rope_tables(pos_ref[0], f_ref[...])
        dil, rows = o_ref.shape[1], o_ref.shape[2]
        w0 = g * 3 * hd
        q = _dot(xp, w_ref[:, w0:w0 + hd])
        q = _rope(_head_rms(q, bd_ref, gq_ref[g]), tabs) * (HEAD_DIM ** -0.5)
        o_ref[0, :, :, 0:hd] = q.astype(BF16).reshape(dil, rows, hd)
        k = _dot(xp, w_ref[:, w0 + hd:w0 + 2 * hd])
        k = _rope(_head_rms(k, bd_ref, gk_ref[g]), tabs)
        o_ref[0, :, :, hd:2 * hd] = k.astype(BF16).reshape(dil, rows, hd)
        v = _dot(xp, w_ref[:, w0 + 2 * hd:w0 + 3 * hd])
        o_ref[0, :, :, 2 * hd:3 * hd] = v.astype(BF16).reshape(dil, rows, hd)


def _a_qkv(h, g, w_qkv, positions, f_row, gq, gk, bd, tm):
    b, s, d = h.shape
    hd = gq.shape[-1]
    dils = [dil for _, dil in DILATED_GROUPS]
    perms = [jnp.asarray(_perm_matrix(tm, dil), BF16) for dil in dils[1:]]
    pos = [positions.reshape(b, s // tm, tm // dil, dil).transpose(0, 1, 3, 2).reshape(b, s, 1) for dil in dils]
    row_spec = lambda last: pl.BlockSpec((1, tm, last), lambda bi, i: (bi, i, 0))
    return pl.pallas_call(
        _a_qkv_kernel,
        grid=(b, s // tm),
        in_specs=[row_spec(d), _resident((1, d)), _resident(w_qkv.shape), _resident((tm, tm)), _resident((tm, tm)),
                  row_spec(1), row_spec(1), row_spec(1), _resident((1, LANES)), _resident(gq.shape),
                  _resident(gk.shape), _resident(bd.shape)],
        out_specs=[pl.BlockSpec((1, dil, tm // dil, 3 * hd), lambda bi, i: (bi, 0, i, 0)) for dil in dils],
        out_shape=[jax.ShapeDtypeStruct((b, dil, s // dil, 3 * hd), BF16) for dil in dils],
        compiler_params=_params(("parallel", "parallel"), 52),
        name="a_qkv",
    )(h, g.reshape(1, d), w_qkv, perms[0], perms[1], pos[0], pos[1], pos[2], f_row, gq, gk, bd)


def _a_attn_kernel(q_ref, kp_ref, kc_ref, vp_ref, vc_ref, o_ref, lse_ref):
    n = pl.program_id(2)
    lane = lax.broadcasted_iota(jnp.int32, (BAND, LANES), 1)
    qi = lax.broadcasted_iota(jnp.int32, (BAND, 2 * BAND), 0)
    kj = lax.broadcasted_iota(jnp.int32, (BAND, 2 * BAND), 1)
    valid = (kj >= qi) & (kj <= qi + BAND) & ((n > 0) | (kj >= BAND))
    lse_tile = jnp.zeros((BAND, LANES), F32)
    for p in range(q_ref.shape[-1] // LANES):
        sl = slice(p * LANES, (p + 1) * LANES)
        q2 = q_ref[0, 0, :, sl]
        kk = jnp.concatenate([kp_ref[0, 0, :, sl], kc_ref[0, 0, :, sl]], axis=0)
        vv = jnp.concatenate([vp_ref[0, 0, :, sl], vc_ref[0, 0, :, sl]], axis=0)
        outs = []
        for half in range(2):
            mine = (lane < HEAD_DIM) if half == 0 else (lane >= HEAD_DIM)
            qh = jnp.where(mine, q2, jnp.zeros_like(q2))
            sc = jnp.where(valid, _dot_nt(qh, kk), NEG)
            m = jnp.max(sc, axis=-1, keepdims=True)
            e = jnp.exp(sc - m)
            l = jnp.sum(e, axis=-1, keepdims=True)
            outs.append(_dot(e.astype(BF16), vv) / l)
            lse_tile = jnp.where(lane == 2 * p + half, m + jnp.log(l), lse_tile)
        o_ref[0, 0, :, sl] = jnp.where(lane < HEAD_DIM, outs[0], outs[1]).astype(BF16)
    lse_ref[0, 0] = lse_tile


def _a_attn(qkv, hd):
    b, dil, sd, _ = qkv.shape
    blk = lambda col, prev: pl.BlockSpec(
        (1, 1, BAND, hd),
        (lambda bi, r, n: (bi, r, jnp.maximum(n - 1, 0), col)) if prev else (lambda bi, r, n: (bi, r, n, col)))
    return pl.pallas_call(
        _a_attn_kernel,
        grid=(b, dil, sd // BAND),
        in_specs=[blk(0, False), blk(1, True), blk(1, False), blk(2, True), blk(2, False)],
        out_specs=[pl.BlockSpec((1, 1, BAND, hd), lambda bi, r, n: (bi, r, n, 0)),
                   pl.BlockSpec((1, 1, BAND, LANES), lambda bi, r, n: (bi, r, n, 0))],
        out_shape=[jax.ShapeDtypeStruct((b, dil, sd, hd), BF16), jax.ShapeDtypeStruct((b, dil, sd, LANES), F32)],
        compiler_params=_params(("parallel", "parallel", "arbitrary"), 32),
        name=f"a_attn_d{dil}",
    )(qkv, qkv, qkv, qkv, qkv)


def _a_mix_kernel(h_ref, o1_ref, o4_ref, o16_ref, l1_ref, l4_ref, l16_ref, pt4_ref, pt16_ref, e_ref, wo_ref,
                  out_ref):
    tm, hd = o1_ref.shape[2], o1_ref.shape[3]

    def unperm_lse(pt_ref, l_ref):
        parts = _split_bf16(l_ref[0].reshape(tm, LANES), 3)
        return sum(_dot(pt_ref[...], part) for part in parts)

    o = [o1_ref[0, 0].astype(F32),
         _dot(pt4_ref[...], o4_ref[0].reshape(tm, hd)),
         _dot(pt16_ref[...], o16_ref[0].reshape(tm, hd))]
    lse = [l1_ref[0, 0], unperm_lse(pt4_ref, l4_ref), unperm_lse(pt16_ref, l16_ref)]
    m = jnp.maximum(jnp.maximum(lse[0], lse[1]), lse[2])
    w = [jnp.exp(v - m) for v in lse]
    inv = 1.0 / (w[0] + w[1] + w[2])
    mixed = jnp.zeros((tm, hd), F32)
    for wg, og in zip(w, o):
        alpha = sum(_dot(part, e_ref[...]) for part in _split_bf16(wg * inv, 2))
        mixed = mixed + alpha * og
    out_ref[0] = h_ref[0] + _dot(mixed.astype(BF16), wo_ref[...])


def _a_mix(h, outs, lses, w_o, tm):
    b, s, d = h.shape
    hd = w_o.shape[0]
    dils = [dil for _, dil in DILATED_GROUPS]
    pts = [jnp.asarray(_perm_matrix(tm, dil).T, BF16) for dil in dils[1:]]
    expand = jnp.asarray((np.arange(LANES)[:, None] == np.arange(hd)[None, :] // HEAD_DIM).astype(np.float32), BF16)
    perm_spec = lambda dil, last: pl.BlockSpec((1, dil, tm // dil, last), lambda bi, i: (bi, 0, i, 0))
    return pl.pallas_call(
        _a_mix_kernel,
        grid=(b, s // tm),
        in_specs=[pl.BlockSpec((1, tm, d), lambda bi, i: (bi, i, 0))]
        + [perm_spec(dil, hd) for dil in dils] + [perm_spec(dil, LANES) for dil in dils]
        + [_resident((tm, tm)), _resident((tm, tm)), _resident(expand.shape), _resident(w_o.shape)],
        out_specs=pl.BlockSpec((1, tm, d), lambda bi, i: (bi, i, 0)),
        out_shape=jax.ShapeDtypeStruct((b, s, d), F32),
        compiler_params=_params(("parallel", "parallel"), 32),
        name="a_mix",
    )(h, *outs, *lses, pts[0], pts[1], expand, w_o)


def _shared_kv_kernel(x_ref, g_ref, wk_ref, wv_ref, wf_ref, bf_ref, gk_ref, bd_ref, tri_ref,
                      k_ref, v_ref, cum_ref, cumt_ref, carry_ref):
    @pl.when(pl.program_id(1) == 0)
    def _():
        carry_ref[...] = jnp.zeros_like(carry_ref)

    xn = _rms(x_ref[0], g_ref[...]).astype(BF16)
    k_ref[0] = _head_rms(_dot(xn, wk_ref[...]), bd_ref, gk_ref[...]).astype(BF16)
    v_ref[0] = _dot(xn, wv_ref[...]).astype(BF16)
    z = _dot(xn, wf_ref[...]) + bf_ref[...]
    log_f = jnp.minimum(z, 0.0) - jnp.log(1.0 + jnp.exp(-jnp.abs(z)))
    cum = carry_ref[...] + sum(_dot(tri_ref[...], part) for part in _split_bf16(log_f, 3))
    carry_ref[...] = cum[-1:, :]
    cum_ref[0] = cum
    cumt_ref[0] = cum.T[:cumt_ref.shape[1], :]


def _shared_kv(h, g, w_kv, b_f, gk, bd, tm=512):
    b, s, d = h.shape
    hd = gk.shape[-1]
    nh = hd // HEAD_DIM
    wk, wv = w_kv[:, :hd].astype(BF16), w_kv[:, hd:2 * hd].astype(BF16)
    wf = jnp.pad(w_kv[:, 2 * hd:], ((0, 0), (0, LANES - nh))).astype(BF16)
    bf = jnp.pad(b_f, (0, LANES - nh)).reshape(1, LANES)
    tri = jnp.asarray(np.tril(np.ones((tm, tm), np.float32)), BF16)
    row = lambda last: pl.BlockSpec((1, tm, last), lambda bi, i: (bi, i, 0))
    return pl.pallas_call(
        _shared_kv_kernel,
        grid=(b, s // tm),
        in_specs=[row(d), _resident((1, d)), _resident(wk.shape), _resident(wv.shape), _resident(wf.shape),
                  _resident(bf.shape), _resident(gk.shape), _resident(bd.shape), _resident(tri.shape)],
        out_specs=[row(hd), row(hd), row(LANES), pl.BlockSpec((1, nh, tm), lambda bi, i: (bi, 0, i))],
        out_shape=[jax.ShapeDtypeStruct((b, s, hd), BF16), jax.ShapeDtypeStruct((b, s, hd), BF16),
                   jax.ShapeDtypeStruct((b, s, LANES), F32), jax.ShapeDtypeStruct((b, nh, s), F32)],
        scratch_shapes=[pltpu.VMEM((1, LANES), F32)],
        compiler_params=_params(("parallel", "arbitrary"), 40),
        name="shared_kv",
    )(h, g.reshape(1, d), wk, wv, wf, bf, gk, bd, tri)


def _b_q_kernel(x_ref, g_ref, w_ref, gq_ref, bd_ref, o_ref):
    xn = _rms(x_ref[...], g_ref[...]).astype(BF16)
    q = _head_rms(_dot(xn, w_ref[...]), bd_ref, gq_ref[...]) * (HEAD_DIM ** -0.5)
    o_ref[...] = q.astype(BF16)


def _b_q(h2, g, w_q, gq, bd, tm=512):
    n, d = h2.shape
    hd = w_q.shape[1]
    return pl.pallas_call(
        _b_q_kernel,
        grid=(n // tm,),
        in_specs=[pl.BlockSpec((tm, d), lambda i: (i, 0)), _resident((1, d)), _resident(w_q.shape),
                  _resident(gq.shape), _resident(bd.shape)],
        out_specs=pl.BlockSpec((tm, hd), lambda i: (i, 0)),
        out_shape=jax.ShapeDtypeStruct((n, hd), BF16),
        compiler_params=_params(("parallel",), 32),
        name="b_q",
    )(h2, g.reshape(1, d), w_q, gq, bd)


def _fox_kernel(q_ref, k_ref, v_ref, cq_ref, ck_ref, o_ref, m_ref, l_ref, acc_ref, *, blk):
    qi, ki = pl.program_id(1), pl.program_id(2)
    n_pairs = q_ref.shape[-1] // LANES

    @pl.when(ki == 0)
    def _():
        m_ref[...] = jnp.full_like(m_ref, NEG)
        l_ref[...] = jnp.zeros_like(l_ref)
        acc_ref[...] = jnp.zeros_like(acc_ref)

    @pl.when(ki <= qi)
    def _():
        row = qi * blk + lax.broadcasted_iota(jnp.int32, (blk, blk), 0)
        col = ki * blk + lax.broadcasted_iota(jnp.int32, (blk, blk), 1)
        causal = col <= row
        lane = lax.broadcasted_iota(jnp.int32, (blk, LANES), 1)
        for p in range(n_pairs):
            sl = slice(p * LANES, (p + 1) * LANES)
            q2, k2, v2 = q_ref[0, :, sl], k_ref[0, :, sl], v_ref[0, :, sl]
            acc_old = acc_ref[:, sl]
            acc_new = []
            for half in range(2):
                hh = 2 * p + half
                mine = (lane < HEAD_DIM) if half == 0 else (lane >= HEAD_DIM)
                qh = jnp.where(mine, q2, jnp.zeros_like(q2))
                bias = cq_ref[0, :, hh:hh + 1] - ck_ref[0, hh:hh + 1, :]
                sc = jnp.where(causal, _dot_nt(qh, k2) + bias, NEG)
                m_old = m_ref[hh]
                m_new = jnp.maximum(m_old, jnp.max(sc, axis=-1, keepdims=True))
                alpha = jnp.exp(m_old - m_new)
                e = jnp.exp(sc - m_new[:, :1])
                l_ref[hh] = alpha * l_ref[hh] + jnp.sum(e, axis=-1, keepdims=True)
                m_ref[hh] = m_new
                acc_new.append(alpha * acc_old + _dot(e.astype(BF16), v2))
            acc_ref[:, sl] = jnp.where(lane < HEAD_DIM, acc_new[0], acc_new[1])

    @pl.when(ki == qi)
    def _():
        lane = lax.broadcasted_iota(jnp.int32, (blk, LANES), 1)
        for p in range(n_pairs):
            sl = slice(p * LANES, (p + 1) * LANES)
            inv = jnp.where(lane < HEAD_DIM, 1.0 / l_ref[2 * p], 1.0 / l_ref[2 * p + 1])
            o_ref[0, :, sl] = (acc_ref[:, sl] * inv).astype(BF16)


def _fox(q, k, v, cum, cum_t, blk=512):
    b, s, hd = q.shape
    nh = hd // HEAD_DIM
    nb = s // blk
    kv_spec = pl.BlockSpec((1, blk, hd), lambda bi, qi, ki: (bi, jnp.minimum(ki, qi), 0))
    return pl.pallas_call(
        functools.partial(_fox_kernel, blk=blk),
        grid=(b, nb, nb),
        in_specs=[pl.BlockSpec((1, blk, hd), lambda bi, qi, ki: (bi, qi, 0)), kv_spec, kv_spec,
                  pl.BlockSpec((1, blk, LANES), lambda bi, qi, ki: (bi, qi, 0)),
                  pl.BlockSpec((1, nh, blk), lambda bi, qi, ki: (bi, 0, jnp.minimum(ki, qi)))],
        out_specs=pl.BlockSpec((1, blk, hd), lambda bi, qi, ki: (bi, qi, 0)),
        out_shape=jax.ShapeDtypeStruct((b, s, hd), BF16),
        scratch_shapes=[pltpu.VMEM((nh, blk, LANES), F32), pltpu.VMEM((nh, blk, LANES), F32),
                        pltpu.VMEM((blk, hd), F32)],
        compiler_params=_params(("parallel", "parallel", "arbitrary"), 48),
        name="fox",
    )(q, k, v, cum, cum_t)


def _out_proj_kernel(h_ref, o_ref, w_ref, out_ref):
    out_ref[...] = h_ref[...] + _dot(o_ref[...], w_ref[...])


def _out_proj(h2, o2, w_o, tm=512):
    n, d = h2.shape
    hd = o2.shape[1]
    return pl.pallas_call(
        _out_proj_kernel,
        grid=(n // tm,),
        in_specs=[pl.BlockSpec((tm, d), lambda i: (i, 0)), pl.BlockSpec((tm, hd), lambda i: (i, 0)),
                  _resident(w_o.shape)],
        out_specs=pl.BlockSpec((tm, d), lambda i: (i, 0)),
        out_shape=jax.ShapeDtypeStruct((n, d), F32),
        compiler_params=_params(("parallel",), 32),
        name="out_proj",
    )(h2, o2, w_o)


def kernel(x, positions, ffn_norm, ffn_w_in, ffn_w_out, mix_norm, a_w_qkv, a_q_norm, a_k_norm, a_w_o,
           kv_norm, kv_w, kv_b_f, kv_k_norm, b_w_q, b_q_norm, b_w_o):
    b, s, d = x.shape
    depth, n_a = ffn_norm.shape[0], a_w_qkv.shape[0]
    hd = a_w_o.shape[1]
    nh = hd // HEAD_DIM
    a_tile = 256

    head_of = np.arange(MXU_DIM) // HEAD_DIM
    bd = jnp.asarray((head_of[:, None] == head_of[None, :]).astype(np.float32) / HEAD_DIM, BF16)
    inv_freq = ROPE_THETA ** (-jnp.arange(0, ROT_DIM, 2, dtype=F32) / ROT_DIM)
    f_head = jnp.concatenate([inv_freq, inv_freq, jnp.zeros((HEAD_DIM - ROT_DIM,), F32)])
    f_row = jnp.tile(f_head, LANES // HEAD_DIM).reshape(1, LANES)
    tile_gain = lambda gain: jnp.tile(gain, nh).reshape(1, hd)

    ffn = lambda h, layer, j: _ffn(h.reshape(b * s, d), ffn_norm[layer, j], ffn_w_in[layer, j].astype(BF16),
                                   ffn_w_out[layer, j].astype(BF16)).reshape(b, s, d)

    h = x
    k_sh = v_sh = cum = cum_t = None
    for layer in range(depth):
        if layer == n_a:
            k_sh, v_sh, cum, cum_t = _shared_kv(h, kv_norm, kv_w, kv_b_f, tile_gain(kv_k_norm), bd)
        h = ffn(h, layer, 0)
        if layer < n_a:
            gq = jnp.stack([tile_gain(a_q_norm[layer, g]) for g in range(len(DILATED_GROUPS))])
            gk = jnp.stack([tile_gain(a_k_norm[layer, g]) for g in range(len(DILATED_GROUPS))])
            qkv = _a_qkv(h, mix_norm[layer], a_w_qkv[layer].astype(BF16), positions, f_row, gq, gk, bd, a_tile)
            attn = [_a_attn(t, hd) for t in qkv]
            h = _a_mix(h, [o for o, _ in attn], [l for _, l in attn], a_w_o[layer].astype(BF16), a_tile)
        else:
            j = layer - n_a
            q = _b_q(h.reshape(b * s, d), mix_norm[layer], b_w_q[j].astype(BF16), tile_gain(b_q_norm[j]), bd)
            o = _fox(q.reshape(b, s, hd), k_sh, v_sh, cum, cum_t)
            h = _out_proj(h.reshape(b * s, d), o.reshape(b * s, hd), b_w_o[j].astype(BF16)).reshape(b, s, d)
        h = ffn(h, layer, 1)
    return h
```

```python
import functools

import numpy as np
import jax
import jax.numpy as jnp
from jax import lax
from jax.experimental import pallas as pl
from jax.experimental.pallas import tpu as pltpu

F32 = jnp.float32
BF16 = jnp.bfloat16

HEAD_DIM = 64
DILATED_GROUPS = ((128, 1), (512, 4), (2048, 16))
BAND = 128
ROT_DIM = HEAD_DIM // 4
ROPE_THETA = 500000.0
EPS = 1e-6
LANES = 128
MXU_DIM = 256
NEG = -1e30
LOG2E = 1.4426950408889634
MIB = 1024 * 1024


def _params(semantics, vmem_mib):
    return pltpu.CompilerParams(dimension_semantics=semantics, vmem_limit_bytes=vmem_mib * MIB)


def _resident(shape):
    nd = len(shape)
    return pl.BlockSpec(shape, lambda *_: (0,) * nd, pipeline_mode=pl.Buffered(1))


def _dot(a, b):
    return jnp.dot(a, b, preferred_element_type=F32)


def _dot_nt(a, b):
    return lax.dot_general(a, b, (((1,), (1,)), ((), ())), preferred_element_type=F32)


def _rms(x, g):
    ms = jnp.mean(x * x, axis=-1, keepdims=True)
    return x * lax.rsqrt(ms + EPS) * g


def _head_rms(y, bd_ref, gain):
    outs = []
    for c in range(y.shape[1] // MXU_DIM):
        yc = y[:, c * MXU_DIM:(c + 1) * MXU_DIM]
        ms = _dot((yc * yc).astype(BF16), bd_ref[...])
        outs.append(yc * lax.rsqrt(ms + EPS))
    return jnp.concatenate(outs, axis=1) * gain


def _rope_tables(pos_col, f_row):
    ang = pos_col.astype(F32) * f_row
    j = lax.broadcasted_iota(jnp.int32, (1, LANES), 1) % HEAD_DIM
    c, s = jnp.cos(ang), jnp.sin(ang)
    half = ROT_DIM // 2
    s_up = jnp.where(j < half, -s, 0.0)
    s_dn = jnp.where((j >= half) & (j < ROT_DIM), s, 0.0)
    return c, s_up, s_dn


def _rope(y, tabs):
    c, s_up, s_dn = tabs
    half = ROT_DIM // 2
    outs = []
    for k in range(y.shape[1] // LANES):
        yc = y[:, k * LANES:(k + 1) * LANES]
        outs.append(yc * c + pltpu.roll(yc, LANES - half, 1) * s_up + pltpu.roll(yc, half, 1) * s_dn)
    return jnp.concatenate(outs, axis=1)


def _split_bf16(v, parts):
    out, r = [], v
    for _ in range(parts):
        p = r.astype(BF16)
        out.append(p)
        r = r - p.astype(F32)
    return out


def _ffn_kernel(x_ref, g_ref, win_ref, wout_ref, o_ref, *, d_ff, chunk):
    x = x_ref[...]
    xn = _rms(x, g_ref[...]).astype(BF16)
    acc = jnp.zeros_like(x)
    for c in range(d_ff // chunk):
        gate = _dot(xn, win_ref[:, c * chunk:(c + 1) * chunk])
        up = _dot(xn, win_ref[:, d_ff + c * chunk:d_ff + (c + 1) * chunk])
        mid = (gate * jax.nn.sigmoid(gate) * up).astype(BF16)
        acc = acc + _dot(mid, wout_ref[c * chunk:(c + 1) * chunk, :])
    o_ref[...] = x + 0.5 * acc


def _ffn(h2, g, w_in, w_out, tm=512):
    n, d = h2.shape
    d_ff = w_out.shape[0]
    chunk = d_ff // 2 if (d_ff // 2) % LANES == 0 else d_ff
    return pl.pallas_call(
        functools.partial(_ffn_kernel, d_ff=d_ff, chunk=chunk),
        grid=(n // tm,),
        in_specs=[pl.BlockSpec((tm, d), lambda i: (i, 0)), _resident((1, d)),
                  _resident(w_in.shape), _resident(w_out.shape)],
        out_specs=pl.BlockSpec((tm, d), lambda i: (i, 0)),
        out_shape=jax.ShapeDtypeStruct((n, d), F32),
        compiler_params=_params(("parallel",), 52),
        name="ffn",
    )(h2, g.reshape(1, d), w_in, w_out)


def _perm_matrix(tm, dil):
    p = np.arange(tm)
    t = (p % (tm // dil)) * dil + p // (tm // dil)
    m = np.zeros((tm, tm), np.float32)
    m[p, t] = 1.0
    return m


def _a_qkv_kernel(x_ref, g_ref, w_ref, p4_ref, p16_ref, pos1_ref, pos4_ref, pos16_ref, f_ref, gq_ref, gk_ref,
                  bd_ref, o1_ref, o4_ref, o16_ref):
    hd = gq_ref.shape[-1]
    xn = _rms(x_ref[0], g_ref[...]).astype(BF16)
    per_group = ((None, pos1_ref, o1_ref), (p4_ref, pos4_ref, o4_ref), (p16_ref, pos16_ref, o16_ref))
    for g, (p_ref, pos_ref, o_ref) in enumerate(per_group):
        xp = xn if p_ref is None else _dot(p_ref[...], xn).astype(BF16)
        tabs = _rope_tables(pos_ref[0], f_ref[...])
        dil, rows = o_ref.shape[1], o_ref.shape[2]
        w0 = g * 3 * hd
        q = _dot(xp, w_ref[:, w0:w0 + hd])
        q = _rope(_head_rms(q, bd_ref, gq_ref[g]), tabs) * (HEAD_DIM ** -0.5)
        o_ref[0, :, :, 0:hd] = q.astype(BF16).reshape(dil, rows, hd)
        k = _dot(xp, w_ref[:, w0 + hd:w0 + 2 * hd])
        k = _rope(_head_rms(k, bd_ref, gk_ref[g]), tabs)
        o_ref[0, :, :, hd:2 * hd] = k.astype(BF16).reshape(dil, rows, hd)
        v = _dot(xp, w_ref[:, w0 + 2 * hd:w0 + 3 * hd])
        o_ref[0, :, :, 2 * hd:3 * hd] = v.astype(BF16).reshape(dil, rows, hd)


def _a_qkv(h, g, w_qkv, positions, f_row, gq, gk, bd, tm):
    b, s, d = h.shape
    hd = gq.shape[-1]
    dils = [dil for _, dil in DILATED_GROUPS]
    perms = [jnp.asarray(_perm_matrix(tm, dil), BF16) for dil in dils[1:]]
    pos = [positions.reshape(b, s // tm, tm // dil, dil).transpose(0, 1, 3, 2).reshape(b, s, 1) for dil in dils]
    row_spec = lambda last: pl.BlockSpec((1, tm, last), lambda bi, i: (bi, i, 0))
    return pl.pallas_call(
        _a_qkv_kernel,
        grid=(b, s // tm),
        in_specs=[row_spec(d), _resident((1, d)), _resident(w_qkv.shape), _resident((tm, tm)), _resident((tm, tm)),
                  row_spec(1), row_spec(1), row_spec(1), _resident((1, LANES)), _resident(gq.shape),
                  _resident(gk.shape), _resident(bd.shape)],
        out_specs=[pl.BlockSpec((1, dil, tm // dil, 3 * hd), lambda bi, i: (bi, 0, i, 0)) for dil in dils],
        out_shape=[jax.ShapeDtypeStruct((b, dil, s // dil, 3 * hd), BF16) for dil in dils],
        compiler_params=_params(("parallel", "parallel"), 52),
        name="a_qkv",
    )(h, g.reshape(1, d), w_qkv, perms[0], perms[1], pos[0], pos[1], pos[2], f_row, gq, gk, bd)


def _a_attn_kernel(q_ref, kp_ref, kc_ref, vp_ref, vc_ref, o_ref, lse_ref):
    n = pl.program_id(2)
    lane = lax.broadcasted_iota(jnp.int32, (BAND, LANES), 1)
    qi = lax.broadcasted_iota(jnp.int32, (BAND, 2 * BAND), 0)
    kj = lax.broadcasted_iota(jnp.int32, (BAND, 2 * BAND), 1)
    valid = (kj >= qi) & (kj <= qi + BAND) & ((n > 0) | (kj >= BAND))
    lse_tile = jnp.zeros((BAND, LANES), F32)
    for p in range(q_ref.shape[-1] // LANES):
        sl = slice(p * LANES, (p + 1) * LANES)
        q2 = q_ref[0, 0, :, sl]
        kk = jnp.concatenate([kp_ref[0, 0, :, sl], kc_ref[0, 0, :, sl]], axis=0)
        vv = jnp.concatenate([vp_ref[0, 0, :, sl], vc_ref[0, 0, :, sl]], axis=0)
        outs = []
        for half in range(2):
            mine = (lane < HEAD_DIM) if half == 0 else (lane >= HEAD_DIM)
            qh = jnp.where(mine, q2, jnp.zeros_like(q2))
            sc = jnp.where(valid, _dot_nt(qh, kk), NEG)
            m = jnp.max(sc, axis=-1, keepdims=True)
            e = jnp.exp(sc - m)
            l = jnp.sum(e, axis=-1, keepdims=True)
            outs.append(_dot(e.astype(BF16), vv) / l)
            lse_tile = jnp.where(lane == 2 * p + half, m + jnp.log(l), lse_tile)
        o_ref[0, 0, :, sl] = jnp.where(lane < HEAD_DIM, outs[0], outs[1]).astype(BF16)
    lse_ref[0, 0] = lse_tile


def _a_attn(qkv, hd):
    b, dil, sd, _ = qkv.shape
    blk = lambda col, prev: pl.BlockSpec(
        (1, 1, BAND, hd),
        (lambda bi, r, n: (bi, r, jnp.maximum(n - 1, 0), col)) if prev else (lambda bi, r, n: (bi, r, n, col)))
    return pl.pallas_call(
        _a_attn_kernel,
        grid=(b, dil, sd // BAND),
        in_specs=[blk(0, False), blk(1, True), blk(1, False), blk(2, True), blk(2, False)],
        out_specs=[pl.BlockSpec((1, 1, BAND, hd), lambda bi, r, n: (bi, r, n, 0)),
                   pl.BlockSpec((1, 1, BAND, LANES), lambda bi, r, n: (bi, r, n, 0))],
        out_shape=[jax.ShapeDtypeStruct((b, dil, sd, hd), BF16), jax.ShapeDtypeStruct((b, dil, sd, LANES), F32)],
        compiler_params=_params(("parallel", "parallel", "arbitrary"), 32),
        name=f"a_attn_d{dil}",
    )(qkv, qkv, qkv, qkv, qkv)


def _a_mix_kernel(h_ref, o1_ref, o4_ref, o16_ref, l1_ref, l4_ref, l16_ref, pt4_ref, pt16_ref, e_ref, wo_ref,
                  out_ref):
    tm, hd = o1_ref.shape[2], o1_ref.shape[3]

    def unperm_lse(pt_ref, l_ref):
        parts = _split_bf16(l_ref[0].reshape(tm, LANES), 3)
        return sum(_dot(pt_ref[...], part) for part in parts)

    o = [o1_ref[0, 0].astype(F32),
         _dot(pt4_ref[...], o4_ref[0].reshape(tm, hd)),
         _dot(pt16_ref[...], o16_ref[0].reshape(tm, hd))]
    lse = [l1_ref[0, 0], unperm_lse(pt4_ref, l4_ref), unperm_lse(pt16_ref, l16_ref)]
    m = jnp.maximum(jnp.maximum(lse[0], lse[1]), lse[2])
    w = [jnp.exp(v - m) for v in lse]
    inv = 1.0 / (w[0] + w[1] + w[2])
    mixed = jnp.zeros((tm, hd), F32)
    for wg, og in zip(w, o):
        alpha = sum(_dot(part, e_ref[...]) for part in _split_bf16(wg * inv, 2))
        mixed = mixed + alpha * og
    out_ref[0] = h_ref[0] + _dot(mixed.astype(BF16), wo_ref[...])


def _a_mix(h, outs, lses, w_o, expand, tm):
    b, s, d = h.shape
    hd = w_o.shape[0]
    dils = [dil for _, dil in DILATED_GROUPS]
    pts = [jnp.asarray(_perm_matrix(tm, dil).T, BF16) for dil in dils[1:]]
    perm_spec = lambda dil, last: pl.BlockSpec((1, dil, tm // dil, last), lambda bi, i: (bi, 0, i, 0))
    return pl.pallas_call(
        _a_mix_kernel,
        grid=(b, s // tm),
        in_specs=[pl.BlockSpec((1, tm, d), lambda bi, i: (bi, i, 0))]
        + [perm_spec(dil, hd) for dil in dils] + [perm_spec(dil, LANES) for dil in dils]
        + [_resident((tm, tm)), _resident((tm, tm)), _resident(expand.shape), _resident(w_o.shape)],
        out_specs=pl.BlockSpec((1, tm, d), lambda bi, i: (bi, i, 0)),
        out_shape=jax.ShapeDtypeStruct((b, s, d), F32),
        compiler_params=_params(("parallel", "parallel"), 32),
        name="a_mix",
    )(h, *outs, *lses, pts[0], pts[1], expand, w_o)


def _store_pairs(ref, y):
    for p in range(ref.shape[1]):
        ref[0, p] = y[:, p * LANES:(p + 1) * LANES].astype(ref.dtype)


def _shared_kv_kernel(x_ref, g_ref, wk_ref, wv_ref, wf_ref, bf_ref, gk_ref, bd_ref, tri_ref, e_ref,
                      k_ref, v_ref, cq_ref, ck_ref, carry_ref):
    @pl.when(pl.program_id(1) == 0)
    def _():
        carry_ref[...] = jnp.zeros_like(carry_ref)

    xn = _rms(x_ref[0], g_ref[...]).astype(BF16)
    _store_pairs(k_ref, _head_rms(_dot(xn, wk_ref[...]), bd_ref, gk_ref[...]))
    _store_pairs(v_ref, _dot(xn, wv_ref[...]))
    z = _dot(xn, wf_ref[...]) + bf_ref[...]
    log_f = jnp.minimum(z, 0.0) - jnp.log(1.0 + jnp.exp(-jnp.abs(z)))
    cum = carry_ref[...] + sum(_dot(tri_ref[...], part) for part in _split_bf16(log_f, 3))
    carry_ref[...] = cum[-1:, :]
    cum2 = cum * LOG2E
    ck_ref[0] = cum2.T[:ck_ref.shape[1], :]
    _store_pairs(cq_ref, sum(_dot(part, e_ref[...]) for part in _split_bf16(cum2, 3)))


def _shared_kv(h, g, w_kv, b_f, gk, bd, expand, tm=512):
    b, s, d = h.shape
    hd = gk.shape[-1]
    nh, n_pairs = hd // HEAD_DIM, hd // LANES
    wk, wv = w_kv[:, :hd].astype(BF16), w_kv[:, hd:2 * hd].astype(BF16)
    wf = jnp.pad(w_kv[:, 2 * hd:], ((0, 0), (0, LANES - nh))).astype(BF16)
    bf = jnp.pad(b_f, (0, LANES - nh)).reshape(1, LANES)
    tri = jnp.asarray(np.tril(np.ones((tm, tm), np.float32)), BF16)
    pairs = lambda: pl.BlockSpec((1, n_pairs, tm, LANES), lambda bi, i: (bi, 0, i, 0))
    return pl.pallas_call(
        _shared_kv_kernel,
        grid=(b, s // tm),
        in_specs=[pl.BlockSpec((1, tm, d), lambda bi, i: (bi, i, 0)), _resident((1, d)), _resident(wk.shape),
                  _resident(wv.shape), _resident(wf.shape), _resident(bf.shape), _resident(gk.shape),
                  _resident(bd.shape), _resident(tri.shape), _resident(expand.shape)],
        out_specs=[pairs(), pairs(), pairs(), pl.BlockSpec((1, nh, tm), lambda bi, i: (bi, 0, i))],
        out_shape=[jax.ShapeDtypeStruct((b, n_pairs, s, LANES), BF16), jax.ShapeDtypeStruct((b, n_pairs, s, LANES), BF16),
                   jax.ShapeDtypeStruct((b, n_pairs, s, LANES), F32), jax.ShapeDtypeStruct((b, nh, s), F32)],
        scratch_shapes=[pltpu.VMEM((1, LANES), F32)],
        compiler_params=_params(("parallel", "arbitrary"), 40),
        name="shared_kv",
    )(h, g.reshape(1, d), wk, wv, wf, bf, gk, bd, tri, expand)


def _b_q_kernel(x_ref, g_ref, w_ref, gq_ref, bd_ref, o_ref):
    xn = _rms(x_ref[0], g_ref[...]).astype(BF16)
    q = _head_rms(_dot(xn, w_ref[...]), bd_ref, gq_ref[...]) * (HEAD_DIM ** -0.5 * LOG2E)
    _store_pairs(o_ref, q)


def _b_q(h, g, w_q, gq, bd, tm=512):
    b, s, d = h.shape
    n_pairs = w_q.shape[1] // LANES
    return pl.pallas_call(
        _b_q_kernel,
        grid=(b, s // tm),
        in_specs=[pl.BlockSpec((1, tm, d), lambda bi, i: (bi, i, 0)), _resident((1, d)), _resident(w_q.shape),
                  _resident(gq.shape), _resident(bd.shape)],
        out_specs=pl.BlockSpec((1, n_pairs, tm, LANES), lambda bi, i: (bi, 0, i, 0)),
        out_shape=jax.ShapeDtypeStruct((b, n_pairs, s, LANES), BF16),
        compiler_params=_params(("parallel", "parallel"), 32),
        name="b_q",
    )(h, g.reshape(1, d), w_q, gq, bd)


def _fox_kernel(q_ref, k_ref, v_ref, cq_ref, ck_ref, o_ref, s_scr, mx_scr, l_scr, acc_scr, *, tq, ck):
    s_len = q_ref.shape[2]
    nj = ck // LANES
    lane = lax.broadcasted_iota(jnp.int32, (tq, LANES), 1)
    lo_half = lane < HEAD_DIM

    def fold(op, t):
        out = t[:, :LANES]
        for j in range(1, nj):
            out = op(out, t[:, j * LANES:(j + 1) * LANES])
        return out

    def q_tile(i, carry):
        r0 = pl.multiple_of(i * tq, tq)
        q2 = q_ref[0, 0, pl.ds(r0, tq), :]
        cq2 = cq_ref[0, 0, pl.ds(r0, tq), :]
        qh = (jnp.where(lo_half, q2, jnp.zeros_like(q2)), jnp.where(lo_half, jnp.zeros_like(q2), q2))
        ct = (cq2[:, 0:1], cq2[:, HEAD_DIM:HEAD_DIM + 1])
        diag = (i * tq) // ck
        mx_scr[...] = jnp.full_like(mx_scr, NEG)

        def scores(c, masked):
            k0 = pl.multiple_of(c * ck, ck)
            kc = k_ref[0, 0, pl.ds(k0, ck), :]
            for half in range(2):
                bias = ct[half] - ck_ref[0, half, pl.ds(c, 1), :]
                sc = _dot_nt(qh[half], kc) + bias
                if masked:
                    row = r0 + lax.broadcasted_iota(jnp.int32, (tq, ck), 0)
                    col = k0 + lax.broadcasted_iota(jnp.int32, (tq, ck), 1)
                    sc = jnp.where(col <= row, sc, NEG)
                s_scr[half, c] = sc
                mx_scr[half] = jnp.maximum(mx_scr[half], fold(jnp.maximum, sc))

        def pass1(c, carry1):
            scores(c, False)
            return carry1

        lax.fori_loop(0, diag, pass1, 0)
        scores(diag, True)
        for half in range(2):
            m = jnp.max(mx_scr[half], axis=-1, keepdims=True)
            mx_scr[half] = jnp.broadcast_to(m, (tq, LANES))
        l_scr[...] = jnp.zeros_like(l_scr)
        acc_scr[...] = jnp.zeros_like(acc_scr)

        def pass2(c, carry2):
            k0 = pl.multiple_of(c * ck, ck)
            vc = v_ref[0, 0, pl.ds(k0, ck), :]
            for half in range(2):
                mb = mx_scr[half]
                sc = s_scr[half, c]
                e = jnp.concatenate([jnp.exp2(sc[:, j * LANES:(j + 1) * LANES] - mb) for j in range(nj)], axis=1)
                l_scr[half] = l_scr[half] + fold(jnp.add, e)
                acc_scr[half] = acc_scr[half] + _dot(e.astype(BF16), vc)
            return carry2

        lax.fori_loop(0, diag + 1, pass2, 0)
        inv = [1.0 / jnp.sum(l_scr[half], axis=-1, keepdims=True) for half in range(2)]
        o = jnp.where(lo_half, acc_scr[0] * inv[0], acc_scr[1] * inv[1])
        o_ref[0, 0, pl.ds(r0, tq), :] = o.astype(BF16)
        return carry

    lax.fori_loop(0, s_len // tq, q_tile, 0)


def _fox(q, k, v, cq, ck_rows, tq=512, ck=512):
    b, n_pairs, s, _ = q.shape
    nh = ck_rows.shape[1]
    ck4 = ck_rows.reshape(b, nh, s // ck, ck)
    pair = lambda: pl.BlockSpec((1, 1, s, LANES), lambda bi, p: (bi, p, 0, 0))
    return pl.pallas_call(
        functools.partial(_fox_kernel, tq=tq, ck=ck),
        grid=(b, n_pairs),
        in_specs=[pair(), pair(), pair(), pair(), pl.BlockSpec((1, 2, s // ck, ck), lambda bi, p: (bi, p, 0, 0))],
        out_specs=pair(),
        out_shape=jax.ShapeDtypeStruct((b, n_pairs, s, LANES), BF16),
        scratch_shapes=[pltpu.VMEM((2, s // ck, tq, ck), F32), pltpu.VMEM((2, tq, LANES), F32),
                        pltpu.VMEM((2, tq, LANES), F32), pltpu.VMEM((2, tq, LANES), F32)],
        compiler_params=_params(("parallel", "parallel"), 48),
        name="fox",
    )(q, k, v, cq, ck4)


def _out_proj_kernel(h_ref, o_ref, w_ref, out_ref):
    o = jnp.concatenate([o_ref[0, p] for p in range(o_ref.shape[1])], axis=1)
    out_ref[0] = h_ref[0] + _dot(o, w_ref[...])


def _out_proj(h, o, w_o, tm=512):
    b, s, d = h.shape
    n_pairs = o.shape[1]
    return pl.pallas_call(
        _out_proj_kernel,
        grid=(b, s // tm),
        in_specs=[pl.BlockSpec((1, tm, d), lambda bi, i: (bi, i, 0)),
                  pl.BlockSpec((1, n_pairs, tm, LANES), lambda bi, i: (bi, 0, i, 0)), _resident(w_o.shape)],
        out_specs=pl.BlockSpec((1, tm, d), lambda bi, i: (bi, i, 0)),
        out_shape=jax.ShapeDtypeStruct((b, s, d), F32),
        compiler_params=_params(("parallel", "parallel"), 32),
        name="out_proj",
    )(h, o, w_o)


def kernel(x, positions, ffn_norm, ffn_w_in, ffn_w_out, mix_norm, a_w_qkv, a_q_norm, a_k_norm, a_w_o,
           kv_norm, kv_w, kv_b_f, kv_k_norm, b_w_q, b_q_norm, b_w_o):
    b, s, d = x.shape
    depth, n_a = ffn_norm.shape[0], a_w_qkv.shape[0]
    hd = a_w_o.shape[1]
    nh = hd // HEAD_DIM
    a_tile = 256

    head_of = np.arange(MXU_DIM) // HEAD_DIM
    bd = jnp.asarray((head_of[:, None] == head_of[None, :]).astype(np.float32) / HEAD_DIM, BF16)
    inv_freq = ROPE_THETA ** (-jnp.arange(0, ROT_DIM, 2, dtype=F32) / ROT_DIM)
    f_head = jnp.concatenate([inv_freq, inv_freq, jnp.zeros((HEAD_DIM - ROT_DIM,), F32)])
    f_row = jnp.tile(f_head, LANES // HEAD_DIM).reshape(1, LANES)
    tile_gain = lambda gain: jnp.tile(gain, nh).reshape(1, hd)
    expand = jnp.asarray((np.arange(LANES)[:, None] == np.arange(hd)[None, :] // HEAD_DIM).astype(np.float32), BF16)

    ffn = lambda h, layer, j: _ffn(h.reshape(b * s, d), ffn_norm[layer, j], ffn_w_in[layer, j].astype(BF16),
                                   ffn_w_out[layer, j].astype(BF16)).reshape(b, s, d)

    h = x
    k_sh = v_sh = cq = ck_rows = None
    for layer in range(depth):
        if layer == n_a:
            k_sh, v_sh, cq, ck_rows = _shared_kv(h, kv_norm, kv_w, kv_b_f, tile_gain(kv_k_norm), bd, expand)
        h = ffn(h, layer, 0)
        if layer < n_a:
            gq = jnp.stack([tile_gain(a_q_norm[layer, g]) for g in range(len(DILATED_GROUPS))])
            gk = jnp.stack([tile_gain(a_k_norm[layer, g]) for g in range(len(DILATED_GROUPS))])
            qkv = _a_qkv(h, mix_norm[layer], a_w_qkv[layer].astype(BF16), positions, f_row, gq, gk, bd, a_tile)
            attn = [_a_attn(t, hd) for t in qkv]
            h = _a_mix(h, [o for o, _ in attn], [l for _, l in attn], a_w_o[layer].astype(BF16), expand, a_tile)
        else:
            j = layer - n_a
            q = _b_q(h, mix_norm[layer], b_w_q[j].astype(BF16), tile_gain(b_q_norm[j]), bd)
            o = _fox(q, k_sh, v_sh, cq, ck_rows)
            h = _out_proj(h, o, b_w_o[j].astype(BF16))
        h = ffn(h, layer, 1)
    return h
```

```python
import functools

import numpy as np
import jax
import jax.numpy as jnp
from jax import lax
from jax.experimental import pallas as pl
from jax.experimental.pallas import tpu as pltpu

F32 = jnp.float32
BF16 = jnp.bfloat16

HEAD_DIM = 64
DILATED_GROUPS = ((128, 1), (512, 4), (2048, 16))
BAND = 128
ROT_DIM = HEAD_DIM // 4
ROPE_THETA = 500000.0
EPS = 1e-6
LANES = 128
MXU_DIM = 256
NEG = -1e30
LOG2E = 1.4426950408889634
LN2 = 0.6931471805599453
MIB = 1024 * 1024


def _params(semantics, vmem_mib):
    return pltpu.CompilerParams(dimension_semantics=semantics, vmem_limit_bytes=vmem_mib * MIB)


def _resident(shape):
    nd = len(shape)
    return pl.BlockSpec(shape, lambda *_: (0,) * nd, pipeline_mode=pl.Buffered(1))


def _dot(a, b):
    return jnp.dot(a, b, preferred_element_type=F32)


def _dot_nt(a, b):
    return lax.dot_general(a, b, (((1,), (1,)), ((), ())), preferred_element_type=F32)


def _rms(x, g):
    ms = jnp.mean(x * x, axis=-1, keepdims=True)
    return x * lax.rsqrt(ms + EPS) * g


def _head_rms(y, bd_ref, gain):
    outs = []
    for c in range(y.shape[1] // MXU_DIM):
        yc = y[:, c * MXU_DIM:(c + 1) * MXU_DIM]
        ms = _dot((yc * yc).astype(BF16), bd_ref[...])
        outs.append(yc * lax.rsqrt(ms + EPS))
    return jnp.concatenate(outs, axis=1) * gain


def _rope_tables(pos_col, f_row):
    ang = pos_col.astype(F32) * f_row
    j = lax.broadcasted_iota(jnp.int32, (1, LANES), 1) % HEAD_DIM
    c, s = jnp.cos(ang), jnp.sin(ang)
    half = ROT_DIM // 2
    s_up = jnp.where(j < half, -s, 0.0)
    s_dn = jnp.where((j >= half) & (j < ROT_DIM), s, 0.0)
    return c, s_up, s_dn


def _rope(y, tabs):
    c, s_up, s_dn = tabs
    half = ROT_DIM // 2
    outs = []
    for k in range(y.shape[1] // LANES):
        yc = y[:, k * LANES:(k + 1) * LANES]
        outs.append(yc * c + pltpu.roll(yc, LANES - half, 1) * s_up + pltpu.roll(yc, half, 1) * s_dn)
    return jnp.concatenate(outs, axis=1)


def _split_bf16(v, parts):
    out, r = [], v
    for _ in range(parts):
        p = r.astype(BF16)
        out.append(p)
        r = r - p.astype(F32)
    return out


def _ffn_kernel(x_ref, g_ref, win_ref, wout_ref, o_ref, *, d_ff, bounds):
    x = x_ref[...]
    xn = _rms(x, g_ref[...]).astype(BF16)
    acc = jnp.zeros_like(x)
    for lo, hi in zip(bounds[:-1], bounds[1:]):
        gate = _dot(xn, win_ref[:, lo:hi])
        up = _dot(xn, win_ref[:, d_ff + lo:d_ff + hi])
        mid = (gate * jax.nn.sigmoid(gate) * up).astype(BF16)
        acc = acc + _dot(mid, wout_ref[lo:hi, :])
    o_ref[...] = x + 0.5 * acc


def _ffn(h2, g, w_in, w_out, tm=512):
    n, d = h2.shape
    d_ff = w_out.shape[0]
    tiles = pl.cdiv(d_ff, MXU_DIM)
    bounds = (0, min(d_ff, pl.cdiv(tiles, 2) * MXU_DIM), d_ff) if tiles > 1 else (0, d_ff)
    return pl.pallas_call(
        functools.partial(_ffn_kernel, d_ff=d_ff, bounds=bounds),
        grid=(n // tm,),
        in_specs=[pl.BlockSpec((tm, d), lambda i: (i, 0)), _resident((1, d)),
                  _resident(w_in.shape), _resident(w_out.shape)],
        out_specs=pl.BlockSpec((tm, d), lambda i: (i, 0)),
        out_shape=jax.ShapeDtypeStruct((n, d), F32),
        compiler_params=_params(("parallel",), 52),
        name="ffn",
    )(h2, g.reshape(1, d), w_in, w_out)


def _perm_matrix(tm, dil):
    p = np.arange(tm)
    t = (p % (tm // dil)) * dil + p // (tm // dil)
    m = np.zeros((tm, tm), np.float32)
    m[p, t] = 1.0
    return m


def _a_qkv_kernel(x_ref, g_ref, w_ref, p4_ref, p16_ref, pos1_ref, pos4_ref, pos16_ref, f_ref, gq_ref, gk_ref,
                  bd_ref, o1_ref, o4_ref, o16_ref):
    hd = gq_ref.shape[-1]
    xn = _rms(x_ref[0], g_ref[...]).astype(BF16)
    per_group = ((None, pos1_ref, o1_ref), (p4_ref, pos4_ref, o4_ref), (p16_ref, pos16_ref, o16_ref))
    for g, (p_ref, pos_ref, o_ref) in enumerate(per_group):
        xp = xn if p_ref is None else _dot(p_ref[...], xn).astype(BF16)
        tabs = _rope_tables(pos_ref[0], f_ref[...])
        dil, rows = o_ref.shape[1], o_ref.shape[2]
        w0 = g * 3 * hd
        q = _dot(xp, w_ref[:, w0:w0 + hd])
        q = _rope(_head_rms(q, bd_ref, gq_ref[g]), tabs) * (HEAD_DIM ** -0.5 * LOG2E)
        o_ref[0, :, :, 0:hd] = q.astype(BF16).reshape(dil, rows, hd)
        k = _dot(xp, w_ref[:, w0 + hd:w0 + 2 * hd])
        k = _rope(_head_rms(k, bd_ref, gk_ref[g]), tabs)
        o_ref[0, :, :, hd:2 * hd] = k.astype(BF16).reshape(dil, rows, hd)
        v = _dot(xp, w_ref[:, w0 + 2 * hd:w0 + 3 * hd])
        o_ref[0, :, :, 2 * hd:3 * hd] = v.astype(BF16).reshape(dil, rows, hd)


def _a_qkv(h, g, w_qkv, positions, f_row, gq, gk, bd, tm):
    b, s, d = h.shape
    hd = gq.shape[-1]
    dils = [dil for _, dil in DILATED_GROUPS]
    perms = [jnp.asarray(_perm_matrix(tm, dil), BF16) for dil in dils[1:]]
    pos = [positions.reshape(b, s // tm, tm // dil, dil).transpose(0, 1, 3, 2).reshape(b, s, 1) for dil in dils]
    row_spec = lambda last: pl.BlockSpec((1, tm, last), lambda bi, i: (bi, i, 0))
    return pl.pallas_call(
        _a_qkv_kernel,
        grid=(b, s // tm),
        in_specs=[row_spec(d), _resident((1, d)), _resident(w_qkv.shape), _resident((tm, tm)), _resident((tm, tm)),
                  row_spec(1), row_spec(1), row_spec(1), _resident((1, LANES)), _resident(gq.shape),
                  _resident(gk.shape), _resident(bd.shape)],
        out_specs=[pl.BlockSpec((1, dil, tm // dil, 3 * hd), lambda bi, i: (bi, 0, i, 0)) for dil in dils],
        out_shape=[jax.ShapeDtypeStruct((b, dil, s // dil, 3 * hd), BF16) for dil in dils],
        compiler_params=_params(("parallel", "parallel"), 52),
        name="a_qkv",
    )(h, g.reshape(1, d), w_qkv, perms[0], perms[1], pos[0], pos[1], pos[2], f_row, gq, gk, bd)


def _a_attn_kernel(q_ref, kp_ref, kc_ref, vp_ref, vc_ref, o_ref, lse_ref):
    n = pl.program_id(2)
    lane = lax.broadcasted_iota(jnp.int32, (BAND, LANES), 1)
    qi = lax.broadcasted_iota(jnp.int32, (BAND, 2 * BAND), 0)
    kj = lax.broadcasted_iota(jnp.int32, (BAND, 2 * BAND), 1)
    valid = (kj >= qi) & (kj <= qi + BAND) & ((n > 0) | (kj >= BAND))
    lse_tile = jnp.zeros((BAND, LANES), F32)
    for p in range(q_ref.shape[-1] // LANES):
        sl = slice(p * LANES, (p + 1) * LANES)
        q2 = q_ref[0, 0, :, sl]
        kk = jnp.concatenate([kp_ref[0, 0, :, sl], kc_ref[0, 0, :, sl]], axis=0)
        vv = jnp.concatenate([vp_ref[0, 0, :, sl], vc_ref[0, 0, :, sl]], axis=0)
        outs = []
        for half in range(2):
            mine = (lane < HEAD_DIM) if half == 0 else (lane >= HEAD_DIM)
            qh = jnp.where(mine, q2, jnp.zeros_like(q2))
            sc = jnp.where(valid, _dot_nt(qh, kk), NEG)
            m = jnp.max(sc, axis=-1, keepdims=True)
            e = jnp.exp2(sc - m)
            l = jnp.sum(e, axis=-1, keepdims=True)
            outs.append(_dot(e.astype(BF16), vv) * (1.0 / l))
            lse_tile = jnp.where(lane == 2 * p + half, m * LN2 + jnp.log(l), lse_tile)
        o_ref[0, 0, :, sl] = jnp.where(lane < HEAD_DIM, outs[0], outs[1]).astype(BF16)
    lse_ref[0, 0] = lse_tile


def _a_attn(qkv, hd):
    b, dil, sd, _ = qkv.shape
    blk = lambda col, prev: pl.BlockSpec(
        (1, 1, BAND, hd),
        (lambda bi, r, n: (bi, r, jnp.maximum(n - 1, 0), col)) if prev else (lambda bi, r, n: (bi, r, n, col)))
    return pl.pallas_call(
        _a_attn_kernel,
        grid=(b, dil, sd // BAND),
        in_specs=[blk(0, False), blk(1, True), blk(1, False), blk(2, True), blk(2, False)],
        out_specs=[pl.BlockSpec((1, 1, BAND, hd), lambda bi, r, n: (bi, r, n, 0)),
                   pl.BlockSpec((1, 1, BAND, LANES), lambda bi, r, n: (bi, r, n, 0))],
        out_shape=[jax.ShapeDtypeStruct((b, dil, sd, hd), BF16), jax.ShapeDtypeStruct((b, dil, sd, LANES), F32)],
        compiler_params=_params(("parallel", "parallel", "arbitrary"), 32),
        name=f"a_attn_d{dil}",
    )(qkv, qkv, qkv, qkv, qkv)


def _a_mix_kernel(h_ref, o1_ref, o4_ref, o16_ref, l1_ref, l4_ref, l16_ref, pt4_ref, pt16_ref, e_ref, wo_ref,
                  out_ref):
    tm, hd = o1_ref.shape[2], o1_ref.shape[3]

    def unperm_lse(pt_ref, l_ref):
        parts = _split_bf16(l_ref[0].reshape(tm, LANES), 3)
        return sum(_dot(pt_ref[...], part) for part in parts)

    o = [o1_ref[0, 0].astype(F32),
         _dot(pt4_ref[...], o4_ref[0].reshape(tm, hd)),
         _dot(pt16_ref[...], o16_ref[0].reshape(tm, hd))]
    lse = [l1_ref[0, 0], unperm_lse(pt4_ref, l4_ref), unperm_lse(pt16_ref, l16_ref)]
    m = jnp.maximum(jnp.maximum(lse[0], lse[1]), lse[2])
    w = [jnp.exp(v - m) for v in lse]
    inv = 1.0 / (w[0] + w[1] + w[2])
    mixed = jnp.zeros((tm, hd), F32)
    for wg, og in zip(w, o):
        alpha = _dot((wg * inv).astype(BF16), e_ref[...])
        mixed = mixed + alpha * og
    out_ref[0] = h_ref[0] + _dot(mixed.astype(BF16), wo_ref[...])


def _a_mix(h, outs, lses, w_o, expand, tm):
    b, s, d = h.shape
    hd = w_o.shape[0]
    dils = [dil for _, dil in DILATED_GROUPS]
    pts = [jnp.asarray(_perm_matrix(tm, dil).T, BF16) for dil in dils[1:]]
    perm_spec = lambda dil, last: pl.BlockSpec((1, dil, tm // dil, last), lambda bi, i: (bi, 0, i, 0))
    return pl.pallas_call(
        _a_mix_kernel,
        grid=(b, s // tm),
        in_specs=[pl.BlockSpec((1, tm, d), lambda bi, i: (bi, i, 0))]
        + [perm_spec(dil, hd) for dil in dils] + [perm_spec(dil, LANES) for dil in dils]
        + [_resident((tm, tm)), _resident((tm, tm)), _resident(expand.shape), _resident(w_o.shape)],
        out_specs=pl.BlockSpec((1, tm, d), lambda bi, i: (bi, i, 0)),
        out_shape=jax.ShapeDtypeStruct((b, s, d), F32),
        compiler_params=_params(("parallel", "parallel"), 32),
        name="a_mix",
    )(h, *outs, *lses, pts[0], pts[1], expand, w_o)


def _store_pairs(ref, y):
    for p in range(ref.shape[1]):
        ref[0, p] = y[:, p * LANES:(p + 1) * LANES].astype(ref.dtype)


def _shared_kv_kernel(x_ref, g_ref, wk_ref, wv_ref, wf_ref, bf_ref, gk_ref, bd_ref, tri_ref, e_ref,
                      k_ref, v_ref, cq_ref, ck_ref, carry_ref):
    @pl.when(pl.program_id(1) == 0)
    def _():
        carry_ref[...] = jnp.zeros_like(carry_ref)

    xn = _rms(x_ref[0], g_ref[...]).astype(BF16)
    _store_pairs(k_ref, _head_rms(_dot(xn, wk_ref[...]), bd_ref, gk_ref[...]))
    _store_pairs(v_ref, _dot(xn, wv_ref[...]))
    z = _dot(xn, wf_ref[...]) + bf_ref[...]
    log_f = jnp.minimum(z, 0.0) - jnp.log(1.0 + jnp.exp(-jnp.abs(z)))
    cum = carry_ref[...] + sum(_dot(tri_ref[...], part) for part in _split_bf16(log_f, 3))
    carry_ref[...] = cum[-1:, :]
    cum2 = cum * LOG2E
    ck_ref[0] = cum2.T[:ck_ref.shape[1], :]
    _store_pairs(cq_ref, _dot(cum2.astype(BF16), e_ref[...]))


def _shared_kv(h, g, w_kv, b_f, gk, bd, expand, tm=512):
    b, s, d = h.shape
    hd = gk.shape[-1]
    nh, n_pairs = hd // HEAD_DIM, hd // LANES
    wk, wv = w_kv[:, :hd].astype(BF16), w_kv[:, hd:2 * hd].astype(BF16)
    wf = jnp.pad(w_kv[:, 2 * hd:], ((0, 0), (0, LANES - nh))).astype(BF16)
    bf = jnp.pad(b_f, (0, LANES - nh)).reshape(1, LANES)
    tri = jnp.asarray(np.tril(np.ones((tm, tm), np.float32)), BF16)
    pairs = lambda: pl.BlockSpec((1, n_pairs, tm, LANES), lambda bi, i: (bi, 0, i, 0))
    return pl.pallas_call(
        _shared_kv_kernel,
        grid=(b, s // tm),
        in_specs=[pl.BlockSpec((1, tm, d), lambda bi, i: (bi, i, 0)), _resident((1, d)), _resident(wk.shape),
                  _resident(wv.shape), _resident(wf.shape), _resident(bf.shape), _resident(gk.shape),
                  _resident(bd.shape), _resident(tri.shape), _resident(expand.shape)],
        out_specs=[pairs(), pairs(), pairs(), pl.BlockSpec((1, nh, tm), lambda bi, i: (bi, 0, i))],
        out_shape=[jax.ShapeDtypeStruct((b, n_pairs, s, LANES), BF16), jax.ShapeDtypeStruct((b, n_pairs, s, LANES), BF16),
                   jax.ShapeDtypeStruct((b, n_pairs, s, LANES), F32), jax.ShapeDtypeStruct((b, nh, s), F32)],
        scratch_shapes=[pltpu.VMEM((1, LANES), F32)],
        compiler_params=_params(("parallel", "arbitrary"), 40),
        name="shared_kv",
    )(h, g.reshape(1, d), wk, wv, wf, bf, gk, bd, tri, expand)


def _b_q_kernel(x_ref, g_ref, w_ref, gq_ref, bd_ref, o_ref):
    xn = _rms(x_ref[0], g_ref[...]).astype(BF16)
    q = _head_rms(_dot(xn, w_ref[...]), bd_ref, gq_ref[...]) * (HEAD_DIM ** -0.5 * LOG2E)
    _store_pairs(o_ref, q)


def _b_q(h, g, w_q, gq, bd, tm=512):
    b, s, d = h.shape
    n_pairs = w_q.shape[1] // LANES
    return pl.pallas_call(
        _b_q_kernel,
        grid=(b, s // tm),
        in_specs=[pl.BlockSpec((1, tm, d), lambda bi, i: (bi, i, 0)), _resident((1, d)), _resident(w_q.shape),
                  _resident(gq.shape), _resident(bd.shape)],
        out_specs=pl.BlockSpec((1, n_pairs, tm, LANES), lambda bi, i: (bi, 0, i, 0)),
        out_shape=jax.ShapeDtypeStruct((b, n_pairs, s, LANES), BF16),
        compiler_params=_params(("parallel", "parallel"), 32),
        name="b_q",
    )(h, g.reshape(1, d), w_q, gq, bd)


def _fox_kernel(q_ref, k_ref, v_ref, cq_ref, ck_ref, o_ref, s_scr, mask_scr, mx_scr, mb_scr, l_scr, acc_scr, *, blk):
    n_tiles = q_ref.shape[2] // blk
    nj = blk // LANES
    lo_half = lax.broadcasted_iota(jnp.int32, (blk, LANES), 1) < HEAD_DIM
    row = lax.broadcasted_iota(jnp.int32, (blk, blk), 0)
    col = lax.broadcasted_iota(jnp.int32, (blk, blk), 1)
    mask_scr[...] = jnp.where(col <= row, 0.0, NEG)

    def fold(op, t):
        out = t[:, :LANES]
        for j in range(1, nj):
            out = op(out, t[:, j * LANES:(j + 1) * LANES])
        return out

    def load_q(i):
        r0 = pl.multiple_of(i * blk, blk)
        q2 = q_ref[0, 0, pl.ds(r0, blk), :]
        cq2 = cq_ref[0, 0, pl.ds(r0, blk), :]
        qh = (jnp.where(lo_half, q2, jnp.zeros_like(q2)), jnp.where(lo_half, jnp.zeros_like(q2), q2))
        return qh, (cq2[:, 0:1], cq2[:, HEAD_DIM:HEAD_DIM + 1])

    def scores(qh, ct, c, diagonal):
        kc = k_ref[0, 0, pl.ds(pl.multiple_of(c * blk, blk), blk), :]
        for half in range(2):
            bias = ct[half] - ck_ref[0, half, pl.ds(c, 1), :]
            sc = _dot_nt(qh[half], kc) + bias
            if diagonal:
                sc = sc + mask_scr[...]
            s_scr[half, c] = sc
            mx_scr[half] = jnp.maximum(mx_scr[half], fold(jnp.maximum, sc))

    def weights(c):
        vc = v_ref[0, 0, pl.ds(pl.multiple_of(c * blk, blk), blk), :]
        for half in range(2):
            mb = mb_scr[half]
            sc = s_scr[half, c]
            e = jnp.concatenate([jnp.exp2(sc[:, j * LANES:(j + 1) * LANES] - mb) for j in range(nj)], axis=1)
            l_scr[half] = l_scr[half] + fold(jnp.add, e)
            acc_scr[half] = acc_scr[half] + _dot(e.astype(BF16), vc)

    def finish_scores():
        for half in range(2):
            mb_scr[half] = jnp.broadcast_to(jnp.max(mx_scr[half], axis=-1, keepdims=True), (blk, LANES))
        mx_scr[...] = jnp.full_like(mx_scr, NEG)
        l_scr[...] = jnp.zeros_like(l_scr)
        acc_scr[...] = jnp.zeros_like(acc_scr)

    def write_out(i):
        inv = [1.0 / jnp.sum(l_scr[half], axis=-1, keepdims=True) for half in range(2)]
        o = jnp.where(lo_half, acc_scr[0] * inv[0], acc_scr[1] * inv[1])
        o_ref[0, 0, pl.ds(pl.multiple_of(i * blk, blk), blk), :] = o.astype(BF16)

    mx_scr[...] = jnp.full_like(mx_scr, NEG)
    scores(*load_q(0), 0, True)
    finish_scores()

    def tile(i, carry):
        qh, ct = load_q(i + 1)

        def both(c, carry1):
            weights(c)
            scores(qh, ct, c, False)
            return carry1

        lax.fori_loop(0, i + 1, both, 0)
        scores(qh, ct, i + 1, True)
        write_out(i)
        finish_scores()
        return carry

    lax.fori_loop(0, n_tiles - 1, tile, 0)

    def last(c, carry1):
        weights(c)
        return carry1

    lax.fori_loop(0, n_tiles, last, 0)
    write_out(n_tiles - 1)


def _fox(q, k, v, cq, ck_rows, blk=512):
    b, n_pairs, s, _ = q.shape
    nh = ck_rows.shape[1]
    ck4 = ck_rows.reshape(b, nh, s // blk, blk)
    pair = lambda: pl.BlockSpec((1, 1, s, LANES), lambda bi, p: (bi, p, 0, 0))
    small = lambda: pltpu.VMEM((2, blk, LANES), F32)
    return pl.pallas_call(
        functools.partial(_fox_kernel, blk=blk),
        grid=(b, n_pairs),
        in_specs=[pair(), pair(), pair(), pair(), pl.BlockSpec((1, 2, s // blk, blk), lambda bi, p: (bi, p, 0, 0))],
        out_specs=pair(),
        out_shape=jax.ShapeDtypeStruct((b, n_pairs, s, LANES), BF16),
        scratch_shapes=[pltpu.VMEM((2, s // blk, blk, blk), F32), pltpu.VMEM((blk, blk), F32),
                        small(), small(), small(), small()],
        compiler_params=_params(("parallel", "parallel"), 48),
        name="fox",
    )(q, k, v, cq, ck4)


def _out_proj_kernel(h_ref, o_ref, w_ref, out_ref):
    o = jnp.concatenate([o_ref[0, p] for p in range(o_ref.shape[1])], axis=1)
    out_ref[0] = h_ref[0] + _dot(o, w_ref[...])


def _out_proj(h, o, w_o, tm=512):
    b, s, d = h.shape
    n_pairs = o.shape[1]
    return pl.pallas_call(
        _out_proj_kernel,
        grid=(b, s // tm),
        in_specs=[pl.BlockSpec((1, tm, d), lambda bi, i: (bi, i, 0)),
                  pl.BlockSpec((1, n_pairs, tm, LANES), lambda bi, i: (bi, 0, i, 0)), _resident(w_o.shape)],
        out_specs=pl.BlockSpec((1, tm, d), lambda bi, i: (bi, i, 0)),
        out_shape=jax.ShapeDtypeStruct((b, s, d), F32),
        compiler_params=_params(("parallel", "parallel"), 32),
        name="out_proj",
    )(h, o, w_o)


def kernel(x, positions, ffn_norm, ffn_w_in, ffn_w_out, mix_norm, a_w_qkv, a_q_norm, a_k_norm, a_w_o,
           kv_norm, kv_w, kv_b_f, kv_k_norm, b_w_q, b_q_norm, b_w_o):
    b, s, d = x.shape
    depth, n_a = ffn_norm.shape[0], a_w_qkv.shape[0]
    hd = a_w_o.shape[1]
    nh = hd // HEAD_DIM
    a_tile = 256

    head_of = np.arange(MXU_DIM) // HEAD_DIM
    bd = jnp.asarray((head_of[:, None] == head_of[None, :]).astype(np.float32) / HEAD_DIM, BF16)
    inv_freq = ROPE_THETA ** (-jnp.arange(0, ROT_DIM, 2, dtype=F32) / ROT_DIM)
    f_head = jnp.concatenate([inv_freq, inv_freq, jnp.zeros((HEAD_DIM - ROT_DIM,), F32)])
    f_row = jnp.tile(f_head, LANES // HEAD_DIM).reshape(1, LANES)
    tile_gain = lambda gain: jnp.tile(gain, nh).reshape(1, hd)
    expand = jnp.asarray((np.arange(LANES)[:, None] == np.arange(hd)[None, :] // HEAD_DIM).astype(np.float32), BF16)

    ffn = lambda h, layer, j: _ffn(h.reshape(b * s, d), ffn_norm[layer, j], ffn_w_in[layer, j].astype(BF16),
                                   ffn_w_out[layer, j].astype(BF16)).reshape(b, s, d)

    h = x
    k_sh = v_sh = cq = ck_rows = None
    for layer in range(depth):
        if layer == n_a:
            k_sh, v_sh, cq, ck_rows = _shared_kv(h, kv_norm, kv_w, kv_b_f, tile_gain(kv_k_norm), bd, expand)
        h = ffn(h, layer, 0)
        if layer < n_a:
            gq = jnp.stack([tile_gain(a_q_norm[layer, g]) for g in range(len(DILATED_GROUPS))])
            gk = jnp.stack([tile_gain(a_k_norm[layer, g]) for g in range(len(DILATED_GROUPS))])
            qkv = _a_qkv(h, mix_norm[layer], a_w_qkv[layer].astype(BF16), positions, f_row, gq, gk, bd, a_tile)
            attn = [_a_attn(t, hd) for t in qkv]
            h = _a_mix(h, [o for o, _ in attn], [l for _, l in attn], a_w_o[layer].astype(BF16), expand, a_tile)
        else:
            j = layer - n_a
            q = _b_q(h, mix_norm[layer], b_w_q[j].astype(BF16), tile_gain(b_q_norm[j]), bd)
            o = _fox(q, k_sh, v_sh, cq, ck_rows)
            h = _out_proj(h, o, b_w_o[j].astype(BF16))
        h = ffn(h, layer, 1)
    return h
```

```python
import functools

import numpy as np
import jax
import jax.numpy as jnp
from jax import lax
from jax.experimental import pallas as pl
from jax.experimental.pallas import tpu as pltpu

F32 = jnp.float32
BF16 = jnp.bfloat16

HEAD_DIM = 64
DILATED_GROUPS = ((128, 1), (512, 4), (2048, 16))
BAND = 128
ROT_DIM = HEAD_DIM // 4
ROPE_THETA = 500000.0
EPS = 1e-6
LANES = 128
MXU_DIM = 256
NEG = -1e30
LOG2E = 1.4426950408889634
LN2 = 0.6931471805599453
MIB = 1024 * 1024


def _params(semantics, vmem_mib):
    return pltpu.CompilerParams(dimension_semantics=semantics, vmem_limit_bytes=vmem_mib * MIB)


def _resident(shape):
    nd = len(shape)
    return pl.BlockSpec(shape, lambda *_: (0,) * nd, pipeline_mode=pl.Buffered(1))


def _dot(a, b):
    return jnp.dot(a, b, preferred_element_type=F32)


def _dot_nt(a, b):
    return lax.dot_general(a, b, (((1,), (1,)), ((), ())), preferred_element_type=F32)


def _rms(x, g):
    ms = jnp.mean(x * x, axis=-1, keepdims=True)
    return x * lax.rsqrt(ms + EPS) * g


def _head_rms(y, bd_ref, gain):
    outs = []
    for c in range(y.shape[1] // MXU_DIM):
        yc = y[:, c * MXU_DIM:(c + 1) * MXU_DIM]
        ms = _dot((yc * yc).astype(BF16), bd_ref[...])
        outs.append(yc * lax.rsqrt(ms + EPS))
    return jnp.concatenate(outs, axis=1) * gain


def _rope_tables(pos_col, f_row):
    ang = pos_col.astype(F32) * f_row
    j = lax.broadcasted_iota(jnp.int32, (1, LANES), 1) % HEAD_DIM
    c, s = jnp.cos(ang), jnp.sin(ang)
    half = ROT_DIM // 2
    s_up = jnp.where(j < half, -s, 0.0)
    s_dn = jnp.where((j >= half) & (j < ROT_DIM), s, 0.0)
    return c, s_up, s_dn


def _rope(y, tabs):
    c, s_up, s_dn = tabs
    half = ROT_DIM // 2
    outs = []
    for k in range(y.shape[1] // LANES):
        yc = y[:, k * LANES:(k + 1) * LANES]
        outs.append(yc * c + pltpu.roll(yc, LANES - half, 1) * s_up + pltpu.roll(yc, half, 1) * s_dn)
    return jnp.concatenate(outs, axis=1)


def _split_bf16(v, parts):
    out, r = [], v
    for _ in range(parts):
        p = r.astype(BF16)
        out.append(p)
        r = r - p.astype(F32)
    return out


def _ffn_kernel(x_ref, g_ref, win_ref, wout_ref, o_ref, *, d_ff, bounds):
    x = x_ref[...]
    xn = _rms(x, g_ref[...]).astype(BF16)
    acc = jnp.zeros_like(x)
    for lo, hi in zip(bounds[:-1], bounds[1:]):
        gate = _dot(xn, win_ref[:, lo:hi])
        up = _dot(xn, win_ref[:, d_ff + lo:d_ff + hi])
        mid = (gate * jax.nn.sigmoid(gate) * up).astype(BF16)
        acc = acc + _dot(mid, wout_ref[lo:hi, :])
    o_ref[...] = x + 0.5 * acc


def _ffn(h2, g, w_in, w_out, tm=512):
    n, d = h2.shape
    d_ff = w_out.shape[0]
    tiles = pl.cdiv(d_ff, MXU_DIM)
    bounds = (0, min(d_ff, pl.cdiv(tiles, 2) * MXU_DIM), d_ff) if tiles > 1 else (0, d_ff)
    return pl.pallas_call(
        functools.partial(_ffn_kernel, d_ff=d_ff, bounds=bounds),
        grid=(n // tm,),
        in_specs=[pl.BlockSpec((tm, d), lambda i: (i, 0)), _resident((1, d)),
                  _resident(w_in.shape), _resident(w_out.shape)],
        out_specs=pl.BlockSpec((tm, d), lambda i: (i, 0)),
        out_shape=jax.ShapeDtypeStruct((n, d), F32),
        compiler_params=_params(("parallel",), 52),
        name="ffn",
    )(h2, g.reshape(1, d), w_in, w_out)


def _perm_matrix(tm, dil):
    p = np.arange(tm)
    t = (p % (tm // dil)) * dil + p // (tm // dil)
    m = np.zeros((tm, tm), np.float32)
    m[p, t] = 1.0
    return m


def _a_qkv_kernel(x_ref, g_ref, w_ref, p4_ref, p16_ref, pos1_ref, pos4_ref, pos16_ref, f_ref, gq_ref, gk_ref,
                  bd_ref, o1_ref, o4_ref, o16_ref):
    hd = gq_ref.shape[-1]
    xn = _rms(x_ref[0], g_ref[...]).astype(BF16)
    per_group = ((None, pos1_ref, o1_ref), (p4_ref, pos4_ref, o4_ref), (p16_ref, pos16_ref, o16_ref))
    for g, (p_ref, pos_ref, o_ref) in enumerate(per_group):
        xp = xn if p_ref is None else _dot(p_ref[...], xn).astype(BF16)
        tabs = _rope_tables(pos_ref[0], f_ref[...])
        dil, rows = o_ref.shape[1], o_ref.shape[2]
        w0 = g * 3 * hd
        q = _dot(xp, w_ref[:, w0:w0 + hd])
        q = _rope(_head_rms(q, bd_ref, gq_ref[g]), tabs) * (HEAD_DIM ** -0.5 * LOG2E)
        o_ref[0, :, :, 0:hd] = q.astype(BF16).reshape(dil, rows, hd)
        k = _dot(xp, w_ref[:, w0 + hd:w0 + 2 * hd])
        k = _rope(_head_rms(k, bd_ref, gk_ref[g]), tabs)
        o_ref[0, :, :, hd:2 * hd] = k.astype(BF16).reshape(dil, rows, hd)
        v = _dot(xp, w_ref[:, w0 + 2 * hd:w0 + 3 * hd])
        o_ref[0, :, :, 2 * hd:3 * hd] = v.astype(BF16).reshape(dil, rows, hd)


def _a_qkv(h, g, w_qkv, positions, f_row, gq, gk, bd, tm):
    b, s, d = h.shape
    hd = gq.shape[-1]
    dils = [dil for _, dil in DILATED_GROUPS]
    perms = [jnp.asarray(_perm_matrix(tm, dil), BF16) for dil in dils[1:]]
    pos = [positions.reshape(b, s // tm, tm // dil, dil).transpose(0, 1, 3, 2).reshape(b, s, 1) for dil in dils]
    row_spec = lambda last: pl.BlockSpec((1, tm, last), lambda bi, i: (bi, i, 0))
    return pl.pallas_call(
        _a_qkv_kernel,
        grid=(b, s // tm),
        in_specs=[row_spec(d), _resident((1, d)), _resident(w_qkv.shape), _resident((tm, tm)), _resident((tm, tm)),
                  row_spec(1), row_spec(1), row_spec(1), _resident((1, LANES)), _resident(gq.shape),
                  _resident(gk.shape), _resident(bd.shape)],
        out_specs=[pl.BlockSpec((1, dil, tm // dil, 3 * hd), lambda bi, i: (bi, 0, i, 0)) for dil in dils],
        out_shape=[jax.ShapeDtypeStruct((b, dil, s // dil, 3 * hd), BF16) for dil in dils],
        compiler_params=_params(("parallel", "parallel"), 52),
        name="a_qkv",
    )(h, g.reshape(1, d), w_qkv, perms[0], perms[1], pos[0], pos[1], pos[2], f_row, gq, gk, bd)


def _a_attn_kernel(q_ref, kp_ref, kc_ref, vp_ref, vc_ref, o_ref, ml_ref):
    n = pl.program_id(2)
    n_heads = q_ref.shape[-1] // HEAD_DIM
    lane = lax.broadcasted_iota(jnp.int32, (BAND, LANES), 1)
    qi = lax.broadcasted_iota(jnp.int32, (BAND, 2 * BAND), 0)
    kj = lax.broadcasted_iota(jnp.int32, (BAND, 2 * BAND), 1)
    valid = (kj >= qi) & (kj <= qi + BAND) & ((n > 0) | (kj >= BAND))
    mask = jnp.where(valid, 0.0, NEG)
    mask = jnp.concatenate([mask, mask], axis=0)
    lo_half = lane < HEAD_DIM
    ml_tile = jnp.zeros((BAND, LANES), F32)
    for p in range(q_ref.shape[-1] // LANES):
        sl = slice(p * LANES, (p + 1) * LANES)
        q2 = q_ref[0, 0, :, sl]
        kk = jnp.concatenate([kp_ref[0, 0, :, sl], kc_ref[0, 0, :, sl]], axis=0)
        vv = jnp.concatenate([vp_ref[0, 0, :, sl], vc_ref[0, 0, :, sl]], axis=0)
        zero = jnp.zeros_like(q2)
        qq = jnp.concatenate([jnp.where(lo_half, q2, zero), jnp.where(lo_half, zero, q2)], axis=0)
        sc = _dot_nt(qq, kk) + mask
        m = jnp.max(sc, axis=-1, keepdims=True)
        e = jnp.exp2(sc - m)
        l = jnp.sum(e, axis=-1, keepdims=True)
        o = _dot(e.astype(BF16), vv)
        o_ref[0, 0, :, sl] = jnp.where(lo_half, o[:BAND], o[BAND:]).astype(BF16)
        for half in range(2):
            rows = slice(half * BAND, (half + 1) * BAND)
            head = 2 * p + half
            ml_tile = jnp.where(lane == head, m[rows], jnp.where(lane == n_heads + head, l[rows], ml_tile))
    ml_ref[0, 0] = ml_tile


def _a_attn(qkv, hd):
    b, dil, sd, _ = qkv.shape
    blk = lambda col, prev: pl.BlockSpec(
        (1, 1, BAND, hd),
        (lambda bi, r, n: (bi, r, jnp.maximum(n - 1, 0), col)) if prev else (lambda bi, r, n: (bi, r, n, col)))
    return pl.pallas_call(
        _a_attn_kernel,
        grid=(b, dil, sd // BAND),
        in_specs=[blk(0, False), blk(1, True), blk(1, False), blk(2, True), blk(2, False)],
        out_specs=[pl.BlockSpec((1, 1, BAND, hd), lambda bi, r, n: (bi, r, n, 0)),
                   pl.BlockSpec((1, 1, BAND, LANES), lambda bi, r, n: (bi, r, n, 0))],
        out_shape=[jax.ShapeDtypeStruct((b, dil, sd, hd), BF16), jax.ShapeDtypeStruct((b, dil, sd, LANES), F32)],
        compiler_params=_params(("parallel", "parallel", "arbitrary"), 32),
        name=f"a_attn_d{dil}",
    )(qkv, qkv, qkv, qkv, qkv)


def _a_mix_kernel(h_ref, o1_ref, o4_ref, o16_ref, l1_ref, l4_ref, l16_ref, pt4_ref, pt16_ref, e_ref, wo_ref,
                  out_ref):
    tm, hd = o1_ref.shape[2], o1_ref.shape[3]
    n_heads = hd // HEAD_DIM

    def unperm_stats(pt_ref, l_ref):
        parts = _split_bf16(l_ref[0].reshape(tm, LANES), 3)
        return sum(_dot(pt_ref[...], part) for part in parts)

    o = [o1_ref[0, 0].astype(F32),
         _dot(pt4_ref[...], o4_ref[0].reshape(tm, hd)),
         _dot(pt16_ref[...], o16_ref[0].reshape(tm, hd))]
    ml = [l1_ref[0, 0], unperm_stats(pt4_ref, l4_ref), unperm_stats(pt16_ref, l16_ref)]
    m = ml
    l = [pltpu.roll(v, LANES - n_heads, 1) for v in ml]
    m_max = jnp.maximum(jnp.maximum(m[0], m[1]), m[2])
    w = [jnp.exp2(v - m_max) for v in m]
    den = w[0] * l[0] + w[1] * l[1] + w[2] * l[2]
    is_head = lax.broadcasted_iota(jnp.int32, (tm, LANES), 1) < n_heads
    mixed = jnp.zeros((tm, hd), F32)
    for wg, og in zip(w, o):
        coef = jnp.where(is_head, wg / den, 0.0).astype(BF16)
        mixed = mixed + _dot(coef, e_ref[...]) * og
    out_ref[0] = h_ref[0] + _dot(mixed.astype(BF16), wo_ref[...])


def _a_mix(h, outs, lses, w_o, expand, tm):
    b, s, d = h.shape
    hd = w_o.shape[0]
    dils = [dil for _, dil in DILATED_GROUPS]
    pts = [jnp.asarray(_perm_matrix(tm, dil).T, BF16) for dil in dils[1:]]
    perm_spec = lambda dil, last: pl.BlockSpec((1, dil, tm // dil, last), lambda bi, i: (bi, 0, i, 0))
    return pl.pallas_call(
        _a_mix_kernel,
        grid=(b, s // tm),
        in_specs=[pl.BlockSpec((1, tm, d), lambda bi, i: (bi, i, 0))]
        + [perm_spec(dil, hd) for dil in dils] + [perm_spec(dil, LANES) for dil in dils]
        + [_resident((tm, tm)), _resident((tm, tm)), _resident(expand.shape), _resident(w_o.shape)],
        out_specs=pl.BlockSpec((1, tm, d), lambda bi, i: (bi, i, 0)),
        out_shape=jax.ShapeDtypeStruct((b, s, d), F32),
        compiler_params=_params(("parallel", "parallel"), 32),
        name="a_mix",
    )(h, *outs, *lses, pts[0], pts[1], expand, w_o)


def _store_pairs(ref, y):
    for p in range(ref.shape[1]):
        ref[0, p] = y[:, p * LANES:(p + 1) * LANES].astype(ref.dtype)


def _shared_kv_kernel(x_ref, g_ref, wk_ref, wv_ref, wf_ref, bf_ref, gk_ref, bd_ref, tri_ref, e_ref,
                      k_ref, v_ref, cq_ref, ck_ref, carry_ref):
    @pl.when(pl.program_id(1) == 0)
    def _():
        carry_ref[...] = jnp.zeros_like(carry_ref)

    xn = _rms(x_ref[0], g_ref[...]).astype(BF16)
    _store_pairs(k_ref, _head_rms(_dot(xn, wk_ref[...]), bd_ref, gk_ref[...]))
    _store_pairs(v_ref, _dot(xn, wv_ref[...]))
    z = _dot(xn, wf_ref[...]) + bf_ref[...]
    log_f = jnp.minimum(z, 0.0) - jnp.log(1.0 + jnp.exp(-jnp.abs(z)))
    cum = carry_ref[...] + sum(_dot(tri_ref[...], part) for part in _split_bf16(log_f, 3))
    carry_ref[...] = cum[-1:, :]
    cum2 = cum * LOG2E
    ck_ref[0] = cum2.T[:ck_ref.shape[1], :]
    _store_pairs(cq_ref, _dot(cum2.astype(BF16), e_ref[...]))


def _shared_kv(h, g, w_kv, b_f, gk, bd, expand, tm=512):
    b, s, d = h.shape
    hd = gk.shape[-1]
    nh, n_pairs = hd // HEAD_DIM, hd // LANES
    wk, wv = w_kv[:, :hd].astype(BF16), w_kv[:, hd:2 * hd].astype(BF16)
    wf = jnp.pad(w_kv[:, 2 * hd:], ((0, 0), (0, LANES - nh))).astype(BF16)
    bf = jnp.pad(b_f, (0, LANES - nh)).reshape(1, LANES)
    tri = jnp.asarray(np.tril(np.ones((tm, tm), np.float32)), BF16)
    pairs = lambda: pl.BlockSpec((1, n_pairs, tm, LANES), lambda bi, i: (bi, 0, i, 0))
    return pl.pallas_call(
        _shared_kv_kernel,
        grid=(b, s // tm),
        in_specs=[pl.BlockSpec((1, tm, d), lambda bi, i: (bi, i, 0)), _resident((1, d)), _resident(wk.shape),
                  _resident(wv.shape), _resident(wf.shape), _resident(bf.shape), _resident(gk.shape),
                  _resident(bd.shape), _resident(tri.shape), _resident(expand.shape)],
        out_specs=[pairs(), pairs(), pairs(), pl.BlockSpec((1, nh, tm), lambda bi, i: (bi, 0, i))],
        out_shape=[jax.ShapeDtypeStruct((b, n_pairs, s, LANES), BF16), jax.ShapeDtypeStruct((b, n_pairs, s, LANES), BF16),
                   jax.ShapeDtypeStruct((b, n_pairs, s, LANES), F32), jax.ShapeDtypeStruct((b, nh, s), F32)],
        scratch_shapes=[pltpu.VMEM((1, LANES), F32)],
        compiler_params=_params(("parallel", "arbitrary"), 40),
        name="shared_kv",
    )(h, g.reshape(1, d), wk, wv, wf, bf, gk, bd, tri, expand)


def _b_q_kernel(x_ref, g_ref, w_ref, gq_ref, bd_ref, o_ref):
    xn = _rms(x_ref[0], g_ref[...]).astype(BF16)
    q = _head_rms(_dot(xn, w_ref[...]), bd_ref, gq_ref[...]) * (HEAD_DIM ** -0.5 * LOG2E)
    _store_pairs(o_ref, q)


def _b_q(h, g, w_q, gq, bd, tm=512):
    b, s, d = h.shape
    n_pairs = w_q.shape[1] // LANES
    return pl.pallas_call(
        _b_q_kernel,
        grid=(b, s // tm),
        in_specs=[pl.BlockSpec((1, tm, d), lambda bi, i: (bi, i, 0)), _resident((1, d)), _resident(w_q.shape),
                  _resident(gq.shape), _resident(bd.shape)],
        out_specs=pl.BlockSpec((1, n_pairs, tm, LANES), lambda bi, i: (bi, 0, i, 0)),
        out_shape=jax.ShapeDtypeStruct((b, n_pairs, s, LANES), BF16),
        compiler_params=_params(("parallel", "parallel"), 32),
        name="b_q",
    )(h, g.reshape(1, d), w_q, gq, bd)


def _fox_kernel(q_ref, k_ref, v_ref, cq_ref, ck_ref, o_ref, s_scr, kt_scr, mask_scr, mx_scr, mb_scr, l_scr, acc_scr,
                *, blk):
    n_tiles = q_ref.shape[2] // blk
    nj = blk // LANES
    lo_half = lax.broadcasted_iota(jnp.int32, (blk, LANES), 1) < HEAD_DIM
    row = lax.broadcasted_iota(jnp.int32, (blk, blk), 0)
    col = lax.broadcasted_iota(jnp.int32, (blk, blk), 1)
    mask_scr[...] = jnp.where(col <= row, 0.0, NEG)
    for c in range(n_tiles):
        kt_scr[c] = k_ref[0, 0, c * blk:(c + 1) * blk, :].T

    def fold(op, t):
        out = t[:, :LANES]
        for j in range(1, nj):
            out = op(out, t[:, j * LANES:(j + 1) * LANES])
        return out

    def load_q(i):
        r0 = pl.multiple_of(i * blk, blk)
        q2 = q_ref[0, 0, pl.ds(r0, blk), :]
        cq2 = cq_ref[0, 0, pl.ds(r0, blk), :]
        qh = (jnp.where(lo_half, q2, jnp.zeros_like(q2)), jnp.where(lo_half, jnp.zeros_like(q2), q2))
        return qh, (cq2[:, 0:1], cq2[:, HEAD_DIM:HEAD_DIM + 1])

    def scores(qh, ct, c, diagonal):
        qk = _dot(jnp.concatenate(qh, axis=0), kt_scr[c])
        for half in range(2):
            bias = ct[half] - ck_ref[0, half, pl.ds(c, 1), :]
            sc = qk[half * blk:(half + 1) * blk] + bias
            if diagonal:
                sc = sc + mask_scr[...]
            s_scr[half, c] = sc
            mx_scr[half] = jnp.maximum(mx_scr[half], fold(jnp.maximum, sc))

    def weights(c):
        vc = v_ref[0, 0, pl.ds(pl.multiple_of(c * blk, blk), blk), :]
        es = []
        for half in range(2):
            mb = mb_scr[half]
            sc = s_scr[half, c]
            e = jnp.concatenate([jnp.exp2(sc[:, j * LANES:(j + 1) * LANES] - mb) for j in range(nj)], axis=1)
            l_scr[half] = l_scr[half] + fold(jnp.add, e)
            es.append(e.astype(BF16))
        pv = _dot(jnp.concatenate(es, axis=0), vc)
        for half in range(2):
            acc_scr[half] = acc_scr[half] + pv[half * blk:(half + 1) * blk]

    def finish_scores():
        for half in range(2):
            mb_scr[half] = jnp.broadcast_to(jnp.max(mx_scr[half], axis=-1, keepdims=True), (blk, LANES))
        mx_scr[...] = jnp.full_like(mx_scr, NEG)
        l_scr[...] = jnp.zeros_like(l_scr)
        acc_scr[...] = jnp.zeros_like(acc_scr)

    def write_out(i):
        inv = [1.0 / jnp.sum(l_scr[half], axis=-1, keepdims=True) for half in range(2)]
        o = jnp.where(lo_half, acc_scr[0] * inv[0], acc_scr[1] * inv[1])
        o_ref[0, 0, pl.ds(pl.multiple_of(i * blk, blk), blk), :] = o.astype(BF16)

    mx_scr[...] = jnp.full_like(mx_scr, NEG)
    scores(*load_q(0), 0, True)
    finish_scores()

    def tile(i, carry):
        qh, ct = load_q(i + 1)
        n_full = i + 1

        def both(c):
            weights(c)
            scores(qh, ct, c, False)

        def two_chunks(c2, carry1):
            both(2 * c2)
            both(2 * c2 + 1)
            return carry1

        lax.fori_loop(0, lax.shift_right_logical(n_full, 1), two_chunks, 0)

        @pl.when((n_full & 1) == 1)
        def _():
            both(n_full - 1)

        scores(qh, ct, n_full, True)
        write_out(i)
        finish_scores()
        return carry

    lax.fori_loop(0, n_tiles - 1, tile, 0)

    def last(c, carry1):
        weights(c)
        return carry1

    lax.fori_loop(0, n_tiles, last, 0, unroll=2)
    write_out(n_tiles - 1)


def _fox(q, k, v, cq, ck_rows, blk=512):
    b, n_pairs, s, _ = q.shape
    nh = ck_rows.shape[1]
    ck4 = ck_rows.reshape(b, nh, s // blk, blk)
    pair = lambda: pl.BlockSpec((1, 1, s, LANES), lambda bi, p: (bi, p, 0, 0))
    small = lambda: pltpu.VMEM((2, blk, LANES), F32)
    return pl.pallas_call(
        functools.partial(_fox_kernel, blk=blk),
        grid=(b, n_pairs),
        in_specs=[pair(), pair(), pair(), pair(), pl.BlockSpec((1, 2, s // blk, blk), lambda bi, p: (bi, p, 0, 0))],
        out_specs=pair(),
        out_shape=jax.ShapeDtypeStruct((b, n_pairs, s, LANES), BF16),
        scratch_shapes=[pltpu.VMEM((2, s // blk, blk, blk), F32), pltpu.VMEM((s // blk, LANES, blk), BF16),
                        pltpu.VMEM((blk, blk), F32),
                        small(), small(), small(), small()],
        compiler_params=_params(("parallel", "parallel"), 48),
        name="fox",
    )(q, k, v, cq, ck4)


def _out_proj_kernel(h_ref, o_ref, w_ref, out_ref):
    o = jnp.concatenate([o_ref[0, p] for p in range(o_ref.shape[1])], axis=1)
    out_ref[0] = h_ref[0] + _dot(o, w_ref[...])


def _out_proj(h, o, w_o, tm=512):
    b, s, d = h.shape
    n_pairs = o.shape[1]
    return pl.pallas_call(
        _out_proj_kernel,
        grid=(b, s // tm),
        in_specs=[pl.BlockSpec((1, tm, d), lambda bi, i: (bi, i, 0)),
                  pl.BlockSpec((1, n_pairs, tm, LANES), lambda bi, i: (bi, 0, i, 0)), _resident(w_o.shape)],
        out_specs=pl.BlockSpec((1, tm, d), lambda bi, i: (bi, i, 0)),
        out_shape=jax.ShapeDtypeStruct((b, s, d), F32),
        compiler_params=_params(("parallel", "parallel"), 32),
        name="out_proj",
    )(h, o, w_o)


def kernel(x, positions, ffn_norm, ffn_w_in, ffn_w_out, mix_norm, a_w_qkv, a_q_norm, a_k_norm, a_w_o,
           kv_norm, kv_w, kv_b_f, kv_k_norm, b_w_q, b_q_norm, b_w_o):
    b, s, d = x.shape
    depth, n_a = ffn_norm.shape[0], a_w_qkv.shape[0]
    hd = a_w_o.shape[1]
    nh = hd // HEAD_DIM
    a_tile = 256

    head_of = np.arange(MXU_DIM) // HEAD_DIM
    bd = jnp.asarray((head_of[:, None] == head_of[None, :]).astype(np.float32) / HEAD_DIM, BF16)
    inv_freq = ROPE_THETA ** (-jnp.arange(0, ROT_DIM, 2, dtype=F32) / ROT_DIM)
    f_head = jnp.concatenate([inv_freq, inv_freq, jnp.zeros((HEAD_DIM - ROT_DIM,), F32)])
    f_row = jnp.tile(f_head, LANES // HEAD_DIM).reshape(1, LANES)
    tile_gain = lambda gain: jnp.tile(gain, nh).reshape(1, hd)
    expand = jnp.asarray((np.arange(LANES)[:, None] == np.arange(hd)[None, :] // HEAD_DIM).astype(np.float32), BF16)

    ffn = lambda h, layer, j: _ffn(h.reshape(b * s, d), ffn_norm[layer, j], ffn_w_in[layer, j].astype(BF16),
                                   ffn_w_out[layer, j].astype(BF16)).reshape(b, s, d)

    h = x
    k_sh = v_sh = cq = ck_rows = None
    for layer in range(depth):
        if layer == n_a:
            k_sh, v_sh, cq, ck_rows = _shared_kv(h, kv_norm, kv_w, kv_b_f, tile_gain(kv_k_norm), bd, expand)
        h = ffn(h, layer, 0)
        if layer < n_a:
            gq = jnp.stack([tile_gain(a_q_norm[layer, g]) for g in range(len(DILATED_GROUPS))])
            gk = jnp.stack([tile_gain(a_k_norm[layer, g]) for g in range(len(DILATED_GROUPS))])
            qkv = _a_qkv(h, mix_norm[layer], a_w_qkv[layer].astype(BF16), positions, f_row, gq, gk, bd, a_tile)
            attn = [_a_attn(t, hd) for t in qkv]
            h = _a_mix(h, [o for o, _ in attn], [l for _, l in attn], a_w_o[layer].astype(BF16), expand, a_tile)
        else:
            j = layer - n_a
            q = _b_q(h, mix_norm[layer], b_w_q[j].astype(BF16), tile_gain(b_q_norm[j]), bd)
            o = _fox(q, k_sh, v_sh, cq, ck_rows)
            h = _out_proj(h, o, b_w_o[j].astype(BF16))
        h = ffn(h, layer, 1)
    return h
```

```python
import functools

import numpy as np
import jax
import jax.numpy as jnp
from jax import lax
from jax.experimental import pallas as pl
from jax.experimental.pallas import tpu as pltpu

F32 = jnp.float32
BF16 = jnp.bfloat16

HEAD_DIM = 64
DILATED_GROUPS = ((128, 1), (512, 4), (2048, 16))
BAND = 128
ROT_DIM = HEAD_DIM // 4
ROPE_THETA = 500000.0
EPS = 1e-6
LANES = 128
MXU_DIM = 256
NEG = -1e30
LOG2E = 1.4426950408889634
LN2 = 0.6931471805599453
V_PAD = 16
MIB = 1024 * 1024


def _params(semantics, vmem_mib):
    return pltpu.CompilerParams(dimension_semantics=semantics, vmem_limit_bytes=vmem_mib * MIB)


def _resident(shape):
    nd = len(shape)
    return pl.BlockSpec(shape, lambda *_: (0,) * nd, pipeline_mode=pl.Buffered(1))


def _dot(a, b):
    return jnp.dot(a, b, preferred_element_type=F32)


def _dot_nt(a, b):
    return lax.dot_general(a, b, (((1,), (1,)), ((), ())), preferred_element_type=F32)


def _rms(x, g):
    ms = jnp.mean(x * x, axis=-1, keepdims=True)
    return x * lax.rsqrt(ms + EPS) * g


def _head_rms(y, bd_ref, gain):
    outs = []
    for c in range(y.shape[1] // MXU_DIM):
        yc = y[:, c * MXU_DIM:(c + 1) * MXU_DIM]
        ms = _dot((yc * yc).astype(BF16), bd_ref[...])
        outs.append(yc * lax.rsqrt(ms + EPS))
    return jnp.concatenate(outs, axis=1) * gain


def _rope_tables(pos_col, f_row):
    ang = pos_col.astype(F32) * f_row
    j = lax.broadcasted_iota(jnp.int32, (1, LANES), 1) % HEAD_DIM
    c, s = jnp.cos(ang), jnp.sin(ang)
    half = ROT_DIM // 2
    s_up = jnp.where(j < half, -s, 0.0)
    s_dn = jnp.where((j >= half) & (j < ROT_DIM), s, 0.0)
    return c, s_up, s_dn


def _rope(y, tabs):
    c, s_up, s_dn = tabs
    half = ROT_DIM // 2
    outs = []
    for k in range(y.shape[1] // LANES):
        yc = y[:, k * LANES:(k + 1) * LANES]
        outs.append(yc * c + pltpu.roll(yc, LANES - half, 1) * s_up + pltpu.roll(yc, half, 1) * s_dn)
    return jnp.concatenate(outs, axis=1)


def _split_bf16(v, parts):
    out, r = [], v
    for _ in range(parts):
        p = r.astype(BF16)
        out.append(p)
        r = r - p.astype(F32)
    return out


def _ffn_kernel(x_ref, g_ref, win_ref, wout_ref, o_ref, *, d_ff, bounds):
    x = x_ref[...]
    xn = _rms(x, g_ref[...]).astype(BF16)
    acc = jnp.zeros_like(x)
    for lo, hi in zip(bounds[:-1], bounds[1:]):
        gate = _dot(xn, win_ref[:, lo:hi])
        up = _dot(xn, win_ref[:, d_ff + lo:d_ff + hi])
        mid = (gate * jax.nn.sigmoid(gate) * up).astype(BF16)
        acc = acc + _dot(mid, wout_ref[lo:hi, :])
    o_ref[...] = x + 0.5 * acc


def _ffn(h2, g, w_in, w_out, tm=512):
    n, d = h2.shape
    d_ff = w_out.shape[0]
    tiles = pl.cdiv(d_ff, MXU_DIM)
    bounds = (0, min(d_ff, pl.cdiv(tiles, 2) * MXU_DIM), d_ff) if tiles > 1 else (0, d_ff)
    return pl.pallas_call(
        functools.partial(_ffn_kernel, d_ff=d_ff, bounds=bounds),
        grid=(n // tm,),
        in_specs=[pl.BlockSpec((tm, d), lambda i: (i, 0)), _resident((1, d)),
                  _resident(w_in.shape), _resident(w_out.shape)],
        out_specs=pl.BlockSpec((tm, d), lambda i: (i, 0)),
        out_shape=jax.ShapeDtypeStruct((n, d), F32),
        compiler_params=_params(("parallel",), 52),
        name="ffn",
    )(h2, g.reshape(1, d), w_in, w_out)


def _perm_matrix(tm, dil):
    p = np.arange(tm)
    t = (p % (tm // dil)) * dil + p // (tm // dil)
    m = np.zeros((tm, tm), np.float32)
    m[p, t] = 1.0
    return m


def _a_qkv_kernel(x_ref, g_ref, w_ref, p4_ref, p16_ref, pos1_ref, pos4_ref, pos16_ref, f_ref, gq_ref, gk_ref,
                  bd_ref, o1_ref, o4_ref, o16_ref):
    hd = gq_ref.shape[-1]
    xn = _rms(x_ref[0], g_ref[...]).astype(BF16)
    per_group = ((None, pos1_ref, o1_ref), (p4_ref, pos4_ref, o4_ref), (p16_ref, pos16_ref, o16_ref))
    for g, (p_ref, pos_ref, o_ref) in enumerate(per_group):
        xp = xn if p_ref is None else _dot(p_ref[...], xn).astype(BF16)
        tabs = _rope_tables(pos_ref[0], f_ref[...])
        dil, rows = o_ref.shape[1], o_ref.shape[2]
        w0 = g * 3 * hd
        q = _dot(xp, w_ref[:, w0:w0 + hd])
        q = _rope(_head_rms(q, bd_ref, gq_ref[g]), tabs) * (HEAD_DIM ** -0.5 * LOG2E)
        o_ref[0, :, :, 0:hd] = q.astype(BF16).reshape(dil, rows, hd)
        k = _dot(xp, w_ref[:, w0 + hd:w0 + 2 * hd])
        k = _rope(_head_rms(k, bd_ref, gk_ref[g]), tabs)
        o_ref[0, :, :, hd:2 * hd] = k.astype(BF16).reshape(dil, rows, hd)
        v = _dot(xp, w_ref[:, w0 + 2 * hd:w0 + 3 * hd])
        o_ref[0, :, :, 2 * hd:3 * hd] = v.astype(BF16).reshape(dil, rows, hd)


def _a_qkv(h, g, w_qkv, positions, f_row, gq, gk, bd, tm):
    b, s, d = h.shape
    hd = gq.shape[-1]
    dils = [dil for _, dil in DILATED_GROUPS]
    perms = [jnp.asarray(_perm_matrix(tm, dil), BF16) for dil in dils[1:]]
    pos = [positions.reshape(b, s // tm, tm // dil, dil).transpose(0, 1, 3, 2).reshape(b, s, 1) for dil in dils]
    row_spec = lambda last: pl.BlockSpec((1, tm, last), lambda bi, i: (bi, i, 0))
    return pl.pallas_call(
        _a_qkv_kernel,
        grid=(b, s // tm),
        in_specs=[row_spec(d), _resident((1, d)), _resident(w_qkv.shape), _resident((tm, tm)), _resident((tm, tm)),
                  row_spec(1), row_spec(1), row_spec(1), _resident((1, LANES)), _resident(gq.shape),
                  _resident(gk.shape), _resident(bd.shape)],
        out_specs=[pl.BlockSpec((1, dil, tm // dil, 3 * hd), lambda bi, i: (bi, 0, i, 0)) for dil in dils],
        out_shape=[jax.ShapeDtypeStruct((b, dil, s // dil, 3 * hd), BF16) for dil in dils],
        compiler_params=_params(("parallel", "parallel"), 52),
        name="a_qkv",
    )(h, g.reshape(1, d), w_qkv, perms[0], perms[1], pos[0], pos[1], pos[2], f_row, gq, gk, bd)


def _a_attn_kernel(q_ref, kp_ref, kc_ref, vp_ref, vc_ref, o_ref, ml_ref):
    n = pl.program_id(2)
    n_heads = q_ref.shape[-1] // HEAD_DIM
    lane = lax.broadcasted_iota(jnp.int32, (BAND, LANES), 1)
    qi = lax.broadcasted_iota(jnp.int32, (BAND, 2 * BAND), 0)
    kj = lax.broadcasted_iota(jnp.int32, (BAND, 2 * BAND), 1)
    valid = (kj >= qi) & (kj <= qi + BAND) & ((n > 0) | (kj >= BAND))
    mask = jnp.where(valid, 0.0, NEG)
    mask = jnp.concatenate([mask, mask], axis=0)
    lo_half = lane < HEAD_DIM
    ml_tile = jnp.zeros((BAND, LANES), F32)
    for p in range(q_ref.shape[-1] // LANES):
        sl = slice(p * LANES, (p + 1) * LANES)
        q2 = q_ref[0, 0, :, sl]
        kk = jnp.concatenate([kp_ref[0, 0, :, sl], kc_ref[0, 0, :, sl]], axis=0)
        vv = jnp.concatenate([vp_ref[0, 0, :, sl], vc_ref[0, 0, :, sl]], axis=0)
        zero = jnp.zeros_like(q2)
        qq = jnp.concatenate([jnp.where(lo_half, q2, zero), jnp.where(lo_half, zero, q2)], axis=0)
        sc = _dot_nt(qq, kk) + mask
        m = jnp.max(sc, axis=-1, keepdims=True)
        e = jnp.exp2(sc - m)
        l = jnp.sum(e, axis=-1, keepdims=True)
        o = _dot(e.astype(BF16), vv)
        o_ref[0, 0, :, sl] = jnp.where(lo_half, o[:BAND], o[BAND:]).astype(BF16)
        for half in range(2):
            rows = slice(half * BAND, (half + 1) * BAND)
            head = 2 * p + half
            ml_tile = jnp.where(lane == head, m[rows], jnp.where(lane == n_heads + head, l[rows], ml_tile))
    ml_ref[0, 0] = ml_tile


def _a_attn(qkv, hd):
    b, dil, sd, _ = qkv.shape
    blk = lambda col, prev: pl.BlockSpec(
        (1, 1, BAND, hd),
        (lambda bi, r, n: (bi, r, jnp.maximum(n - 1, 0), col)) if prev else (lambda bi, r, n: (bi, r, n, col)))
    return pl.pallas_call(
        _a_attn_kernel,
        grid=(b, dil, sd // BAND),
        in_specs=[blk(0, False), blk(1, True), blk(1, False), blk(2, True), blk(2, False)],
        out_specs=[pl.BlockSpec((1, 1, BAND, hd), lambda bi, r, n: (bi, r, n, 0)),
                   pl.BlockSpec((1, 1, BAND, LANES), lambda bi, r, n: (bi, r, n, 0))],
        out_shape=[jax.ShapeDtypeStruct((b, dil, sd, hd), BF16), jax.ShapeDtypeStruct((b, dil, sd, LANES), F32)],
        compiler_params=_params(("parallel", "parallel", "arbitrary"), 32),
        name=f"a_attn_d{dil}",
    )(qkv, qkv, qkv, qkv, qkv)


def _a_mix_kernel(h_ref, o1_ref, o4_ref, o16_ref, l1_ref, l4_ref, l16_ref, pt4_ref, pt16_ref, e_ref, wo_ref,
                  out_ref):
    tm, hd = o1_ref.shape[2], o1_ref.shape[3]
    n_heads = hd // HEAD_DIM

    def unperm_stats(pt_ref, l_ref):
        parts = _split_bf16(l_ref[0].reshape(tm, LANES), 3)
        return sum(_dot(pt_ref[...], part) for part in parts)

    o = [o1_ref[0, 0].astype(F32),
         _dot(pt4_ref[...], o4_ref[0].reshape(tm, hd)),
         _dot(pt16_ref[...], o16_ref[0].reshape(tm, hd))]
    ml = [l1_ref[0, 0], unperm_stats(pt4_ref, l4_ref), unperm_stats(pt16_ref, l16_ref)]
    m = ml
    l = [pltpu.roll(v, LANES - n_heads, 1) for v in ml]
    m_max = jnp.maximum(jnp.maximum(m[0], m[1]), m[2])
    w = [jnp.exp2(v - m_max) for v in m]
    den = w[0] * l[0] + w[1] * l[1] + w[2] * l[2]
    is_head = lax.broadcasted_iota(jnp.int32, (tm, LANES), 1) < n_heads
    mixed = jnp.zeros((tm, hd), F32)
    for wg, og in zip(w, o):
        coef = jnp.where(is_head, wg / den, 0.0).astype(BF16)
        mixed = mixed + _dot(coef, e_ref[...]) * og
    out_ref[0] = h_ref[0] + _dot(mixed.astype(BF16), wo_ref[...])


def _a_mix(h, outs, lses, w_o, expand, tm):
    b, s, d = h.shape
    hd = w_o.shape[0]
    dils = [dil for _, dil in DILATED_GROUPS]
    pts = [jnp.asarray(_perm_matrix(tm, dil).T, BF16) for dil in dils[1:]]
    perm_spec = lambda dil, last: pl.BlockSpec((1, dil, tm // dil, last), lambda bi, i: (bi, 0, i, 0))
    return pl.pallas_call(
        _a_mix_kernel,
        grid=(b, s // tm),
        in_specs=[pl.BlockSpec((1, tm, d), lambda bi, i: (bi, i, 0))]
        + [perm_spec(dil, hd) for dil in dils] + [perm_spec(dil, LANES) for dil in dils]
        + [_resident((tm, tm)), _resident((tm, tm)), _resident(expand.shape), _resident(w_o.shape)],
        out_specs=pl.BlockSpec((1, tm, d), lambda bi, i: (bi, i, 0)),
        out_shape=jax.ShapeDtypeStruct((b, s, d), F32),
        compiler_params=_params(("parallel", "parallel"), 32),
        name="a_mix",
    )(h, *outs, *lses, pts[0], pts[1], expand, w_o)


def _store_pairs(ref, y):
    for p in range(ref.shape[1]):
        ref[0, p] = y[:, p * LANES:(p + 1) * LANES].astype(ref.dtype)


def _shared_kv_kernel(x_ref, g_ref, wk_ref, wv_ref, wf_ref, bf_ref, gk_ref, bd_ref, tri_ref, e_ref,
                      k_ref, v_ref, cq_ref, ck_ref, carry_ref):
    @pl.when(pl.program_id(1) == 0)
    def _():
        carry_ref[...] = jnp.zeros_like(carry_ref)

    xn = _rms(x_ref[0], g_ref[...]).astype(BF16)
    _store_pairs(k_ref, _head_rms(_dot(xn, wk_ref[...]), bd_ref, gk_ref[...]))
    _store_pairs(v_ref, _dot(xn, wv_ref[...]))
    z = _dot(xn, wf_ref[...]) + bf_ref[...]
    log_f = jnp.minimum(z, 0.0) - jnp.log(1.0 + jnp.exp(-jnp.abs(z)))
    cum = carry_ref[...] + sum(_dot(tri_ref[...], part) for part in _split_bf16(log_f, 3))
    carry_ref[...] = cum[-1:, :]
    cum2 = cum * LOG2E
    ck_ref[0] = cum2.T[:ck_ref.shape[1], :]
    _store_pairs(cq_ref, sum(_dot(part, e_ref[...]) for part in _split_bf16(cum2, 3)))


def _shared_kv(h, g, w_kv, b_f, gk, bd, expand, tm=512):
    b, s, d = h.shape
    hd = gk.shape[-1]
    nh, n_pairs = hd // HEAD_DIM, hd // LANES
    wk, wv = w_kv[:, :hd].astype(BF16), w_kv[:, hd:2 * hd].astype(BF16)
    wf = jnp.pad(w_kv[:, 2 * hd:], ((0, 0), (0, LANES - nh))).astype(BF16)
    bf = jnp.pad(b_f, (0, LANES - nh)).reshape(1, LANES)
    tri = jnp.asarray(np.tril(np.ones((tm, tm), np.float32)), BF16)
    pairs = lambda: pl.BlockSpec((1, n_pairs, tm, LANES), lambda bi, i: (bi, 0, i, 0))
    return pl.pallas_call(
        _shared_kv_kernel,
        grid=(b, s // tm),
        in_specs=[pl.BlockSpec((1, tm, d), lambda bi, i: (bi, i, 0)), _resident((1, d)), _resident(wk.shape),
                  _resident(wv.shape), _resident(wf.shape), _resident(bf.shape), _resident(gk.shape),
                  _resident(bd.shape), _resident(tri.shape), _resident(expand.shape)],
        out_specs=[pairs(), pairs(), pairs(), pl.BlockSpec((1, nh, tm), lambda bi, i: (bi, 0, i))],
        out_shape=[jax.ShapeDtypeStruct((b, n_pairs, s, LANES), BF16), jax.ShapeDtypeStruct((b, n_pairs, s, LANES), BF16),
                   jax.ShapeDtypeStruct((b, n_pairs, s, LANES), F32), jax.ShapeDtypeStruct((b, nh, s), F32)],
        scratch_shapes=[pltpu.VMEM((1, LANES), F32)],
        compiler_params=_params(("parallel", "arbitrary"), 40),
        name="shared_kv",
    )(h, g.reshape(1, d), wk, wv, wf, bf, gk, bd, tri, expand)


def _b_q_kernel(x_ref, g_ref, w_ref, gq_ref, bd_ref, o_ref):
    xn = _rms(x_ref[0], g_ref[...]).astype(BF16)
    q = _head_rms(_dot(xn, w_ref[...]), bd_ref, gq_ref[...]) * (HEAD_DIM ** -0.5 * LOG2E)
    _store_pairs(o_ref, q)


def _b_q(h, g, w_q, gq, bd, tm=512):
    b, s, d = h.shape
    n_pairs = w_q.shape[1] // LANES
    return pl.pallas_call(
        _b_q_kernel,
        grid=(b, s // tm),
        in_specs=[pl.BlockSpec((1, tm, d), lambda bi, i: (bi, i, 0)), _resident((1, d)), _resident(w_q.shape),
                  _resident(gq.shape), _resident(bd.shape)],
        out_specs=pl.BlockSpec((1, n_pairs, tm, LANES), lambda bi, i: (bi, 0, i, 0)),
        out_shape=jax.ShapeDtypeStruct((b, n_pairs, s, LANES), BF16),
        compiler_params=_params(("parallel", "parallel"), 32),
        name="b_q",
    )(h, g.reshape(1, d), w_q, gq, bd)


def _fox_kernel(q_ref, k_ref, v_ref, cs_ref, ct_ref, o_ref, s_scr, vt_scr, cs_scr, mask_scr, mx_scr, mb_scr, acc_scr,
                *, blk):
    n_tiles = q_ref.shape[2] // blk
    nj = blk // LANES
    groups = blk // 8
    lo_half = lax.broadcasted_iota(jnp.int32, (blk, LANES), 1) < HEAD_DIM
    lo_rows = lax.broadcasted_iota(jnp.int32, (LANES, blk), 0) < HEAD_DIM
    key = lax.broadcasted_iota(jnp.int32, (blk, blk), 0)
    qry = lax.broadcasted_iota(jnp.int32, (blk, blk), 1)
    mask_scr[...] = jnp.where(key <= qry, 0.0, NEG)
    ones_rows = jnp.where(lax.broadcasted_iota(jnp.int32, (V_PAD, blk), 0) == 0, 1.0, 0.0).astype(BF16)
    for c in range(n_tiles):
        rows = slice(c * blk, (c + 1) * blk)
        vt = v_ref[0, 0, rows, :].T
        for half in range(2):
            vt_scr[c, half, 0:HEAD_DIM] = vt[half * HEAD_DIM:(half + 1) * HEAD_DIM]
            vt_scr[c, half, HEAD_DIM:HEAD_DIM + V_PAD] = ones_rows
        cs2 = cs_ref[0, 0, rows, :]
        swapped = pltpu.roll(cs2, HEAD_DIM, 1)
        cs_scr[0, rows] = jnp.where(lo_half, cs2, swapped)
        cs_scr[1, rows] = jnp.where(lo_half, swapped, cs2)

    def load_q(i):
        qt = q_ref[0, 0, pl.ds(pl.multiple_of(i * blk, blk), blk), :].T
        zero = jnp.zeros_like(qt)
        qq = jnp.concatenate([jnp.where(lo_rows, qt, zero), jnp.where(lo_rows, zero, qt)], axis=1)
        return qq, (ct_ref[0, 0, pl.ds(i, 1), :], ct_ref[0, 1, pl.ds(i, 1), :])

    def scores(qq, ct, c, diagonal):
        r0 = pl.multiple_of(c * blk, blk)
        st = _dot(k_ref[0, 0, pl.ds(r0, blk), :], qq)
        for half in range(2):
            cs = cs_scr[half, pl.ds(r0, blk), :]
            sc = jnp.concatenate(
                [st[:, half * blk + j * LANES:half * blk + (j + 1) * LANES] - cs for j in range(nj)], axis=1)
            if diagonal:
                sc = sc + mask_scr[...]
            s_scr[half, c] = sc
            mx_scr[half] = jnp.maximum(mx_scr[half], jnp.max(sc.reshape(groups, 8, blk), axis=0))

    def weights(c):
        for half in range(2):
            e = jnp.exp2(s_scr[half, c] - mb_scr[half][0:1, :])
            acc_scr[half] = acc_scr[half] + _dot(vt_scr[c, half], e.astype(BF16))

    def finish_scores(ct):
        for half in range(2):
            row_max = jnp.max(mx_scr[half], axis=0, keepdims=True) + ct[half]
            shift = row_max - ct[half]
            mb_scr[half] = jnp.broadcast_to(shift, (8, blk))
        mx_scr[...] = jnp.full_like(mx_scr, NEG)
        acc_scr[...] = jnp.zeros_like(acc_scr)

    def write_out(i):
        ot = jnp.concatenate([acc_scr[half, 0:HEAD_DIM] * (1.0 / acc_scr[half, HEAD_DIM:HEAD_DIM + 1])
                              for half in range(2)], axis=0)
        o_ref[0, 0, pl.ds(pl.multiple_of(i * blk, blk), blk), :] = ot.T.astype(BF16)

    mx_scr[...] = jnp.full_like(mx_scr, NEG)
    qq0, ct0 = load_q(0)
    scores(qq0, ct0, 0, True)
    finish_scores(ct0)

    def tile(i, carry):
        qq, ct = load_q(i + 1)
        n_full = i + 1

        def both(c):
            weights(c)
            scores(qq, ct, c, False)

        def two_chunks(c2, carry1):
            both(2 * c2)
            both(2 * c2 + 1)
            return carry1

        lax.fori_loop(0, lax.shift_right_logical(n_full, 1), two_chunks, 0)

        @pl.when((n_full & 1) == 1)
        def _():
            both(n_full - 1)

        scores(qq, ct, n_full, True)
        write_out(i)
        finish_scores(ct)
        return carry

    lax.fori_loop(0, n_tiles - 1, tile, 0)

    def last(c, carry1):
        weights(c)
        return carry1

    lax.fori_loop(0, n_tiles, last, 0, unroll=2)
    write_out(n_tiles - 1)


def _fox(q, k, v, cs, ct_rows, blk=512):
    b, n_pairs, s, _ = q.shape
    nh = ct_rows.shape[1]
    ct4 = ct_rows.reshape(b, nh, s // blk, blk)
    pair = lambda: pl.BlockSpec((1, 1, s, LANES), lambda bi, p: (bi, p, 0, 0))
    stat = lambda rows: pltpu.VMEM((2, rows, blk), F32)
    return pl.pallas_call(
        functools.partial(_fox_kernel, blk=blk),
        grid=(b, n_pairs),
        in_specs=[pair(), pair(), pair(), pair(), pl.BlockSpec((1, 2, s // blk, blk), lambda bi, p: (bi, p, 0, 0))],
        out_specs=pair(),
        out_shape=jax.ShapeDtypeStruct((b, n_pairs, s, LANES), BF16),
        scratch_shapes=[pltpu.VMEM((2, s // blk, blk, blk), F32),
                        pltpu.VMEM((s // blk, 2, HEAD_DIM + V_PAD, blk), BF16),
                        pltpu.VMEM((2, s, LANES), F32), pltpu.VMEM((blk, blk), F32),
                        stat(8), stat(8), stat(HEAD_DIM + V_PAD)],
        compiler_params=_params(("parallel", "parallel"), 48),
        name="fox",
    )(q, k, v, cs, ct4)


def _out_proj_kernel(h_ref, o_ref, w_ref, out_ref):
    o = jnp.concatenate([o_ref[0, p] for p in range(o_ref.shape[1])], axis=1)
    out_ref[0] = h_ref[0] + _dot(o, w_ref[...])


def _out_proj(h, o, w_o, tm=512):
    b, s, d = h.shape
    n_pairs = o.shape[1]
    return pl.pallas_call(
        _out_proj_kernel,
        grid=(b, s // tm),
        in_specs=[pl.BlockSpec((1, tm, d), lambda bi, i: (bi, i, 0)),
                  pl.BlockSpec((1, n_pairs, tm, LANES), lambda bi, i: (bi, 0, i, 0)), _resident(w_o.shape)],
        out_specs=pl.BlockSpec((1, tm, d), lambda bi, i: (bi, i, 0)),
        out_shape=jax.ShapeDtypeStruct((b, s, d), F32),
        compiler_params=_params(("parallel", "parallel"), 32),
        name="out_proj",
    )(h, o, w_o)


def kernel(x, positions, ffn_norm, ffn_w_in, ffn_w_out, mix_norm, a_w_qkv, a_q_norm, a_k_norm, a_w_o,
           kv_norm, kv_w, kv_b_f, kv_k_norm, b_w_q, b_q_norm, b_w_o):
    b, s, d = x.shape
    depth, n_a = ffn_norm.shape[0], a_w_qkv.shape[0]
    hd = a_w_o.shape[1]
    nh = hd // HEAD_DIM
    a_tile = 256

    head_of = np.arange(MXU_DIM) // HEAD_DIM
    bd = jnp.asarray((head_of[:, None] == head_of[None, :]).astype(np.float32) / HEAD_DIM, BF16)
    inv_freq = ROPE_THETA ** (-jnp.arange(0, ROT_DIM, 2, dtype=F32) / ROT_DIM)
    f_head = jnp.concatenate([inv_freq, inv_freq, jnp.zeros((HEAD_DIM - ROT_DIM,), F32)])
    f_row = jnp.tile(f_head, LANES // HEAD_DIM).reshape(1, LANES)
    tile_gain = lambda gain: jnp.tile(gain, nh).reshape(1, hd)
    expand = jnp.asarray((np.arange(LANES)[:, None] == np.arange(hd)[None, :] // HEAD_DIM).astype(np.float32), BF16)

    ffn = lambda h, layer, j: _ffn(h.reshape(b * s, d), ffn_norm[layer, j], ffn_w_in[layer, j].astype(BF16),
                                   ffn_w_out[layer, j].astype(BF16)).reshape(b, s, d)

    h = x
    k_sh = v_sh = cq = ck_rows = None
    for layer in range(depth):
        if layer == n_a:
            k_sh, v_sh, cq, ck_rows = _shared_kv(h, kv_norm, kv_w, kv_b_f, tile_gain(kv_k_norm), bd, expand)
        h = ffn(h, layer, 0)
        if layer < n_a:
            gq = jnp.stack([tile_gain(a_q_norm[layer, g]) for g in range(len(DILATED_GROUPS))])
            gk = jnp.stack([tile_gain(a_k_norm[layer, g]) for g in range(len(DILATED_GROUPS))])
            qkv = _a_qkv(h, mix_norm[layer], a_w_qkv[layer].astype(BF16), positions, f_row, gq, gk, bd, a_tile)
            attn = [_a_attn(t, hd) for t in qkv]
            h = _a_mix(h, [o for o, _ in attn], [l for _, l in attn], a_w_o[layer].astype(BF16), expand, a_tile)
        else:
            j = layer - n_a
            q = _b_q(h, mix_norm[layer], b_w_q[j].astype(BF16), tile_gain(b_q_norm[j]), bd)
            o = _fox(q, k_sh, v_sh, cq, ck_rows)
            h = _out_proj(h, o, b_w_o[j].astype(BF16))
        h = ffn(h, layer, 1)
    return h
```

```python
import functools

import numpy as np
import jax
import jax.numpy as jnp
from jax import lax
from jax.experimental import pallas as pl
from jax.experimental.pallas import tpu as pltpu

F32 = jnp.float32
BF16 = jnp.bfloat16

HEAD_DIM = 64
DILATED_GROUPS = ((128, 1), (512, 4), (2048, 16))
BAND = 128
ROT_DIM = HEAD_DIM // 4
ROPE_THETA = 500000.0
EPS = 1e-6
LANES = 128
MXU_DIM = 256
NEG = -1e30
LOG2E = 1.4426950408889634
LN2 = 0.6931471805599453
V_PAD = 16
MIB = 1024 * 1024


def _params(semantics, vmem_mib):
    return pltpu.CompilerParams(dimension_semantics=semantics, vmem_limit_bytes=vmem_mib * MIB)


def _resident(shape):
    nd = len(shape)
    return pl.BlockSpec(shape, lambda *_: (0,) * nd, pipeline_mode=pl.Buffered(1))


def _dot(a, b):
    return jnp.dot(a, b, preferred_element_type=F32)


def _dot_nt(a, b):
    return lax.dot_general(a, b, (((1,), (1,)), ((), ())), preferred_element_type=F32)


def _rms(x, g):
    ms = jnp.mean(x * x, axis=-1, keepdims=True)
    return x * lax.rsqrt(ms + EPS) * g


def _head_rms(y, bd_ref, gain):
    outs = []
    for c in range(y.shape[1] // MXU_DIM):
        yc = y[:, c * MXU_DIM:(c + 1) * MXU_DIM]
        ms = _dot((yc * yc).astype(BF16), bd_ref[...])
        outs.append(yc * lax.rsqrt(ms + EPS))
    return jnp.concatenate(outs, axis=1) * gain


def _rope_tables(pos_col, f_row):
    ang = pos_col.astype(F32) * f_row
    j = lax.broadcasted_iota(jnp.int32, (1, LANES), 1) % HEAD_DIM
    c, s = jnp.cos(ang), jnp.sin(ang)
    half = ROT_DIM // 2
    s_up = jnp.where(j < half, -s, 0.0)
    s_dn = jnp.where((j >= half) & (j < ROT_DIM), s, 0.0)
    return c, s_up, s_dn


def _rope(y, tabs):
    c, s_up, s_dn = tabs
    half = ROT_DIM // 2
    outs = []
    for k in range(y.shape[1] // LANES):
        yc = y[:, k * LANES:(k + 1) * LANES]
        outs.append(yc * c + pltpu.roll(yc, LANES - half, 1) * s_up + pltpu.roll(yc, half, 1) * s_dn)
    return jnp.concatenate(outs, axis=1)


def _split_bf16(v, parts):
    out, r = [], v
    for _ in range(parts):
        p = r.astype(BF16)
        out.append(p)
        r = r - p.astype(F32)
    return out


def _ffn_kernel(x_ref, g_ref, win_ref, wout_ref, o_ref, *, d_ff, bounds):
    x = x_ref[...]
    xn = _rms(x, g_ref[...]).astype(BF16)
    acc = jnp.zeros_like(x)
    for lo, hi in zip(bounds[:-1], bounds[1:]):
        gate = _dot(xn, win_ref[:, lo:hi])
        up = _dot(xn, win_ref[:, d_ff + lo:d_ff + hi])
        mid = (gate * jax.nn.sigmoid(gate) * up).astype(BF16)
        acc = acc + _dot(mid, wout_ref[lo:hi, :])
    o_ref[...] = x + 0.5 * acc


def _ffn(h2, g, w_in, w_out, tm=512):
    n, d = h2.shape
    d_ff = w_out.shape[0]
    tiles = pl.cdiv(d_ff, MXU_DIM)
    bounds = (0, min(d_ff, pl.cdiv(tiles, 2) * MXU_DIM), d_ff) if tiles > 1 else (0, d_ff)
    return pl.pallas_call(
        functools.partial(_ffn_kernel, d_ff=d_ff, bounds=bounds),
        grid=(n // tm,),
        in_specs=[pl.BlockSpec((tm, d), lambda i: (i, 0)), _resident((1, d)),
                  _resident(w_in.shape), _resident(w_out.shape)],
        out_specs=pl.BlockSpec((tm, d), lambda i: (i, 0)),
        out_shape=jax.ShapeDtypeStruct((n, d), F32),
        compiler_params=_params(("parallel",), 52),
        name="ffn",
    )(h2, g.reshape(1, d), w_in, w_out)


def _perm_matrix(tm, dil):
    p = np.arange(tm)
    t = (p % (tm // dil)) * dil + p // (tm // dil)
    m = np.zeros((tm, tm), np.float32)
    m[p, t] = 1.0
    return m


def _a_qkv_kernel(x_ref, g_ref, w_ref, p4_ref, p16_ref, pos1_ref, pos4_ref, pos16_ref, f_ref, gq_ref, gk_ref,
                  bd_ref, o1_ref, o4_ref, o16_ref):
    hd = gq_ref.shape[-1]
    xn = _rms(x_ref[0], g_ref[...]).astype(BF16)
    per_group = ((None, pos1_ref, o1_ref), (p4_ref, pos4_ref, o4_ref), (p16_ref, pos16_ref, o16_ref))
    for g, (p_ref, pos_ref, o_ref) in enumerate(per_group):
        xp = xn if p_ref is None else _dot(p_ref[...], xn).astype(BF16)
        tabs = _rope_tables(pos_ref[0], f_ref[...])
        dil, rows = o_ref.shape[1], o_ref.shape[2]
        w0 = g * 3 * hd
        q = _dot(xp, w_ref[:, w0:w0 + hd])
        q = _rope(_head_rms(q, bd_ref, gq_ref[g]), tabs) * (HEAD_DIM ** -0.5 * LOG2E)
        o_ref[0, :, :, 0:hd] = q.astype(BF16).reshape(dil, rows, hd)
        k = _dot(xp, w_ref[:, w0 + hd:w0 + 2 * hd])
        k = _rope(_head_rms(k, bd_ref, gk_ref[g]), tabs)
        o_ref[0, :, :, hd:2 * hd] = k.astype(BF16).reshape(dil, rows, hd)
        v = _dot(xp, w_ref[:, w0 + 2 * hd:w0 + 3 * hd])
        o_ref[0, :, :, 2 * hd:3 * hd] = v.astype(BF16).reshape(dil, rows, hd)


def _a_qkv(h, g, w_qkv, positions, f_row, gq, gk, bd, tm):
    b, s, d = h.shape
    hd = gq.shape[-1]
    dils = [dil for _, dil in DILATED_GROUPS]
    perms = [jnp.asarray(_perm_matrix(tm, dil), BF16) for dil in dils[1:]]
    pos = [positions.reshape(b, s // tm, tm // dil, dil).transpose(0, 1, 3, 2).reshape(b, s, 1) for dil in dils]
    row_spec = lambda last: pl.BlockSpec((1, tm, last), lambda bi, i: (bi, i, 0))
    return pl.pallas_call(
        _a_qkv_kernel,
        grid=(b, s // tm),
        in_specs=[row_spec(d), _resident((1, d)), _resident(w_qkv.shape), _resident((tm, tm)), _resident((tm, tm)),
                  row_spec(1), row_spec(1), row_spec(1), _resident((1, LANES)), _resident(gq.shape),
                  _resident(gk.shape), _resident(bd.shape)],
        out_specs=[pl.BlockSpec((1, dil, tm // dil, 3 * hd), lambda bi, i: (bi, 0, i, 0)) for dil in dils],
        out_shape=[jax.ShapeDtypeStruct((b, dil, s // dil, 3 * hd), BF16) for dil in dils],
        compiler_params=_params(("parallel", "parallel"), 52),
        name="a_qkv",
    )(h, g.reshape(1, d), w_qkv, perms[0], perms[1], pos[0], pos[1], pos[2], f_row, gq, gk, bd)


def _a_attn_kernel(q_ref, kp_ref, kc_ref, vp_ref, vc_ref, o_ref, ml_ref):
    n = pl.program_id(2)
    n_heads = q_ref.shape[-1] // HEAD_DIM
    lane = lax.broadcasted_iota(jnp.int32, (BAND, LANES), 1)
    qi = lax.broadcasted_iota(jnp.int32, (BAND, 2 * BAND), 0)
    kj = lax.broadcasted_iota(jnp.int32, (BAND, 2 * BAND), 1)
    band = (kj >= qi) & (kj <= qi + BAND)
    lo_half = lane < HEAD_DIM
    for res, j in [(res, j) for res in range(q_ref.shape[1]) for j in range(q_ref.shape[2] // BAND)]:
        rows = slice(j * BAND, (j + 1) * BAND)
        before = slice((j - 1) * BAND, j * BAND)
        valid = band & ((n > 0) | (kj >= BAND)) if j == 0 else band
        mask = jnp.where(valid, 0.0, NEG)
        mask = jnp.concatenate([mask, mask], axis=0)
        ml_tile = jnp.zeros((BAND, LANES), F32)
        for p in range(q_ref.shape[-1] // LANES):
            sl = slice(p * LANES, (p + 1) * LANES)
            q2 = q_ref[0, res, rows, sl]
            k_prev = kp_ref[0, res, :, sl] if j == 0 else kc_ref[0, res, before, sl]
            v_prev = vp_ref[0, res, :, sl] if j == 0 else vc_ref[0, res, before, sl]
            kk = jnp.concatenate([k_prev, kc_ref[0, res, rows, sl]], axis=0)
            vv = jnp.concatenate([v_prev, vc_ref[0, res, rows, sl]], axis=0)
            zero = jnp.zeros_like(q2)
            qq = jnp.concatenate([jnp.where(lo_half, q2, zero), jnp.where(lo_half, zero, q2)], axis=0)
            sc = _dot_nt(qq, kk) + mask
            m = jnp.max(sc, axis=-1, keepdims=True)
            e = jnp.exp2(sc - m)
            l = jnp.sum(e, axis=-1, keepdims=True)
            o = _dot(e.astype(BF16), vv)
            o_ref[0, res, rows, sl] = jnp.where(lo_half, o[:BAND], o[BAND:]).astype(BF16)
            for half in range(2):
                hrows = slice(half * BAND, (half + 1) * BAND)
                head = 2 * p + half
                ml_tile = jnp.where(lane == head, m[hrows], jnp.where(lane == n_heads + head, l[hrows], ml_tile))
        ml_ref[0, res, rows, :] = ml_tile


def _a_attn(qkv, hd, blocks_per_step=8):
    b, dil, sd, _ = qkv.shape
    seq_blocks = min(blocks_per_step, sd // BAND)
    res = min(dil, blocks_per_step // seq_blocks)
    rows = BAND * seq_blocks
    cur = lambda col: pl.BlockSpec((1, res, rows, hd), lambda bi, r, n: (bi, r, n, col))
    prev = lambda col: pl.BlockSpec(
        (1, res, BAND, hd), lambda bi, r, n: (bi, r, jnp.maximum(n * seq_blocks - 1, 0), col))
    return pl.pallas_call(
        _a_attn_kernel,
        grid=(b, dil // res, sd // rows),
        in_specs=[cur(0), prev(1), cur(1), prev(2), cur(2)],
        out_specs=[pl.BlockSpec((1, res, rows, hd), lambda bi, r, n: (bi, r, n, 0)),
                   pl.BlockSpec((1, res, rows, LANES), lambda bi, r, n: (bi, r, n, 0))],
        out_shape=[jax.ShapeDtypeStruct((b, dil, sd, hd), BF16), jax.ShapeDtypeStruct((b, dil, sd, LANES), F32)],
        compiler_params=_params(("parallel", "parallel", "arbitrary"), 32),
        name=f"a_attn_d{dil}",
    )(qkv, qkv, qkv, qkv, qkv)


def _a_mix_kernel(h_ref, o1_ref, o4_ref, o16_ref, l1_ref, l4_ref, l16_ref, pt4_ref, pt16_ref, e_ref, wo_ref,
                  out_ref):
    tm, hd = o1_ref.shape[2], o1_ref.shape[3]
    n_heads = hd // HEAD_DIM

    def unperm_stats(pt_ref, l_ref):
        parts = _split_bf16(l_ref[0].reshape(tm, LANES), 3)
        return sum(_dot(pt_ref[...], part) for part in parts)

    o = [o1_ref[0, 0].astype(F32),
         _dot(pt4_ref[...], o4_ref[0].reshape(tm, hd)),
         _dot(pt16_ref[...], o16_ref[0].reshape(tm, hd))]
    ml = [l1_ref[0, 0], unperm_stats(pt4_ref, l4_ref), unperm_stats(pt16_ref, l16_ref)]
    m = ml
    l = [pltpu.roll(v, LANES - n_heads, 1) for v in ml]
    m_max = jnp.maximum(jnp.maximum(m[0], m[1]), m[2])
    w = [jnp.exp2(v - m_max) for v in m]
    den = w[0] * l[0] + w[1] * l[1] + w[2] * l[2]
    is_head = lax.broadcasted_iota(jnp.int32, (tm, LANES), 1) < n_heads
    mixed = jnp.zeros((tm, hd), F32)
    for wg, og in zip(w, o):
        coef = jnp.where(is_head, wg / den, 0.0).astype(BF16)
        mixed = mixed + _dot(coef, e_ref[...]) * og
    out_ref[0] = h_ref[0] + _dot(mixed.astype(BF16), wo_ref[...])


def _a_mix(h, outs, lses, w_o, expand, tm):
    b, s, d = h.shape
    hd = w_o.shape[0]
    dils = [dil for _, dil in DILATED_GROUPS]
    pts = [jnp.asarray(_perm_matrix(tm, dil).T, BF16) for dil in dils[1:]]
    perm_spec = lambda dil, last: pl.BlockSpec((1, dil, tm // dil, last), lambda bi, i: (bi, 0, i, 0))
    return pl.pallas_call(
        _a_mix_kernel,
        grid=(b, s // tm),
        in_specs=[pl.BlockSpec((1, tm, d), lambda bi, i: (bi, i, 0))]
        + [perm_spec(dil, hd) for dil in dils] + [perm_spec(dil, LANES) for dil in dils]
        + [_resident((tm, tm)), _resident((tm, tm)), _resident(expand.shape), _resident(w_o.shape)],
        out_specs=pl.BlockSpec((1, tm, d), lambda bi, i: (bi, i, 0)),
        out_shape=jax.ShapeDtypeStruct((b, s, d), F32),
        compiler_params=_params(("parallel", "parallel"), 32),
        name="a_mix",
    )(h, *outs, *lses, pts[0], pts[1], expand, w_o)


def _store_pairs(ref, y):
    for p in range(ref.shape[1]):
        ref[0, p] = y[:, p * LANES:(p + 1) * LANES].astype(ref.dtype)


def _shared_kv_kernel(x_ref, g_ref, wk_ref, wv_ref, wf_ref, bf_ref, gk_ref, bd_ref, tri_ref, e_ref,
                      k_ref, v_ref, cq_ref, ck_ref, carry_ref):
    @pl.when(pl.program_id(1) == 0)
    def _():
        carry_ref[...] = jnp.zeros_like(carry_ref)

    xn = _rms(x_ref[0], g_ref[...]).astype(BF16)
    _store_pairs(k_ref, _head_rms(_dot(xn, wk_ref[...]), bd_ref, gk_ref[...]))
    _store_pairs(v_ref, _dot(xn, wv_ref[...]))
    z = _dot(xn, wf_ref[...]) + bf_ref[...]
    log_f = jnp.minimum(z, 0.0) - jnp.log(1.0 + jnp.exp(-jnp.abs(z)))
    cum = carry_ref[...] + sum(_dot(tri_ref[...], part) for part in _split_bf16(log_f, 3))
    carry_ref[...] = cum[-1:, :]
    cum2 = cum * LOG2E
    ck_ref[0] = cum2.T[:ck_ref.shape[1], :]
    _store_pairs(cq_ref, sum(_dot(part, e_ref[...]) for part in _split_bf16(cum2, 3)))


def _shared_kv(h, g, w_kv, b_f, gk, bd, expand, tm=512):
    b, s, d = h.shape
    hd = gk.shape[-1]
    nh, n_pairs = hd // HEAD_DIM, hd // LANES
    wk, wv = w_kv[:, :hd].astype(BF16), w_kv[:, hd:2 * hd].astype(BF16)
    wf = jnp.pad(w_kv[:, 2 * hd:], ((0, 0), (0, LANES - nh))).astype(BF16)
    bf = jnp.pad(b_f, (0, LANES - nh)).reshape(1, LANES)
    tri = jnp.asarray(np.tril(np.ones((tm, tm), np.float32)), BF16)
    pairs = lambda: pl.BlockSpec((1, n_pairs, tm, LANES), lambda bi, i: (bi, 0, i, 0))
    return pl.pallas_call(
        _shared_kv_kernel,
        grid=(b, s // tm),
        in_specs=[pl.BlockSpec((1, tm, d), lambda bi, i: (bi, i, 0)), _resident((1, d)), _resident(wk.shape),
                  _resident(wv.shape), _resident(wf.shape), _resident(bf.shape), _resident(gk.shape),
                  _resident(bd.shape), _resident(tri.shape), _resident(expand.shape)],
        out_specs=[pairs(), pairs(), pairs(), pl.BlockSpec((1, nh, tm), lambda bi, i: (bi, 0, i))],
        out_shape=[jax.ShapeDtypeStruct((b, n_pairs, s, LANES), BF16), jax.ShapeDtypeStruct((b, n_pairs, s, LANES), BF16),
                   jax.ShapeDtypeStruct((b, n_pairs, s, LANES), F32), jax.ShapeDtypeStruct((b, nh, s), F32)],
        scratch_shapes=[pltpu.VMEM((1, LANES), F32)],
        compiler_params=_params(("parallel", "arbitrary"), 40),
        name="shared_kv",
    )(h, g.reshape(1, d), wk, wv, wf, bf, gk, bd, tri, expand)


def _b_q_kernel(x_ref, g_ref, w_ref, gq_ref, bd_ref, o_ref):
    xn = _rms(x_ref[0], g_ref[...]).astype(BF16)
    q = _head_rms(_dot(xn, w_ref[...]), bd_ref, gq_ref[...]) * (HEAD_DIM ** -0.5 * LOG2E)
    _store_pairs(o_ref, q)


def _b_q(h, g, w_q, gq, bd, tm=512):
    b, s, d = h.shape
    n_pairs = w_q.shape[1] // LANES
    return pl.pallas_call(
        _b_q_kernel,
        grid=(b, s // tm),
        in_specs=[pl.BlockSpec((1, tm, d), lambda bi, i: (bi, i, 0)), _resident((1, d)), _resident(w_q.shape),
                  _resident(gq.shape), _resident(bd.shape)],
        out_specs=pl.BlockSpec((1, n_pairs, tm, LANES), lambda bi, i: (bi, 0, i, 0)),
        out_shape=jax.ShapeDtypeStruct((b, n_pairs, s, LANES), BF16),
        compiler_params=_params(("parallel", "parallel"), 32),
        name="b_q",
    )(h, g.reshape(1, d), w_q, gq, bd)


def _fox_kernel(q_ref, k_ref, v_ref, cs_ref, ct_ref, o_ref, s_scr, vt_scr, cs_scr, mask_scr, mx_scr, mb_scr, acc_scr,
                *, blk):
    n_tiles = q_ref.shape[2] // blk
    nj = blk // LANES
    groups = blk // 8
    lo_half = lax.broadcasted_iota(jnp.int32, (blk, LANES), 1) < HEAD_DIM
    lo_rows = lax.broadcasted_iota(jnp.int32, (LANES, blk), 0) < HEAD_DIM
    key = lax.broadcasted_iota(jnp.int32, (blk, blk), 0)
    qry = lax.broadcasted_iota(jnp.int32, (blk, blk), 1)
    mask_scr[...] = jnp.where(key <= qry, 0.0, NEG)
    ones_rows = jnp.where(lax.broadcasted_iota(jnp.int32, (V_PAD, blk), 0) == 0, 1.0, 0.0).astype(BF16)
    for c in range(n_tiles):
        rows = slice(c * blk, (c + 1) * blk)
        vt = v_ref[0, 0, rows, :].T
        for half in range(2):
            vt_scr[c, half, 0:HEAD_DIM] = vt[half * HEAD_DIM:(half + 1) * HEAD_DIM]
            vt_scr[c, half, HEAD_DIM:HEAD_DIM + V_PAD] = ones_rows
        cs2 = cs_ref[0, 0, rows, :]
        swapped = pltpu.roll(cs2, HEAD_DIM, 1)
        cs_scr[0, rows] = jnp.where(lo_half, cs2, swapped)
        cs_scr[1, rows] = jnp.where(lo_half, swapped, cs2)

    def load_q(i):
        qt = q_ref[0, 0, pl.ds(pl.multiple_of(i * blk, blk), blk), :].T
        zero = jnp.zeros_like(qt)
        qq = jnp.concatenate([jnp.where(lo_rows, qt, zero), jnp.where(lo_rows, zero, qt)], axis=1)
        return qq, (ct_ref[0, 0, pl.ds(i, 1), :], ct_ref[0, 1, pl.ds(i, 1), :])

    def scores(qq, ct, c, diagonal):
        r0 = pl.multiple_of(c * blk, blk)
        st = _dot(k_ref[0, 0, pl.ds(r0, blk), :], qq)
        for half in range(2):
            cs = cs_scr[half, pl.ds(r0, blk), :]
            sc = jnp.concatenate(
                [st[:, half * blk + j * LANES:half * blk + (j + 1) * LANES] - cs for j in range(nj)], axis=1)
            if diagonal:
                sc = sc + mask_scr[...]
            s_scr[half, c] = sc
            mx_scr[half] = jnp.maximum(mx_scr[half], jnp.max(sc.reshape(groups, 8, blk), axis=0))

    def weights(c):
        for half in range(2):
            e = jnp.exp2(s_scr[half, c] - mb_scr[half][0:1, :])
            acc_scr[half] = acc_scr[half] + _dot(vt_scr[c, half], e.astype(BF16))

    def finish_scores(ct):
        for half in range(2):
            row_max = jnp.max(mx_scr[half], axis=0, keepdims=True) + ct[half]
            shift = row_max - ct[half]
            mb_scr[half] = jnp.broadcast_to(shift, (8, blk))
        mx_scr[...] = jnp.full_like(mx_scr, NEG)
        acc_scr[...] = jnp.zeros_like(acc_scr)

    def write_out(i):
        ot = jnp.concatenate([acc_scr[half, 0:HEAD_DIM] * (1.0 / acc_scr[half, HEAD_DIM:HEAD_DIM + 1])
                              for half in range(2)], axis=0)
        o_ref[0, 0, pl.ds(pl.multiple_of(i * blk, blk), blk), :] = ot.T.astype(BF16)

    mx_scr[...] = jnp.full_like(mx_scr, NEG)
    qq0, ct0 = load_q(0)
    scores(qq0, ct0, 0, True)
    finish_scores(ct0)

    def tile(i, carry):
        qq, ct = load_q(i + 1)
        n_full = i + 1

        def both(c):
            weights(c)
            scores(qq, ct, c, False)

        def two_chunks(c2, carry1):
            both(2 * c2)
            both(2 * c2 + 1)
            return carry1

        lax.fori_loop(0, lax.shift_right_logical(n_full, 1), two_chunks, 0)

        @pl.when((n_full & 1) == 1)
        def _():
            both(n_full - 1)

        scores(qq, ct, n_full, True)
        write_out(i)
        finish_scores(ct)
        return carry

    lax.fori_loop(0, n_tiles - 1, tile, 0)

    def last(c, carry1):
        weights(c)
        return carry1

    lax.fori_loop(0, n_tiles, last, 0, unroll=2)
    write_out(n_tiles - 1)


def _fox(q, k, v, cs, ct_rows, blk=512):
    b, n_pairs, s, _ = q.shape
    nh = ct_rows.shape[1]
    ct4 = ct_rows.reshape(b, nh, s // blk, blk)
    pair = lambda: pl.BlockSpec((1, 1, s, LANES), lambda bi, p: (bi, p, 0, 0))
    stat = lambda rows: pltpu.VMEM((2, rows, blk), F32)
    return pl.pallas_call(
        functools.partial(_fox_kernel, blk=blk),
        grid=(b, n_pairs),
        in_specs=[pair(), pair(), pair(), pair(), pl.BlockSpec((1, 2, s // blk, blk), lambda bi, p: (bi, p, 0, 0))],
        out_specs=pair(),
        out_shape=jax.ShapeDtypeStruct((b, n_pairs, s, LANES), BF16),
        scratch_shapes=[pltpu.VMEM((2, s // blk, blk, blk), F32),
                        pltpu.VMEM((s // blk, 2, HEAD_DIM + V_PAD, blk), BF16),
                        pltpu.VMEM((2, s, LANES), F32), pltpu.VMEM((blk, blk), F32),
                        stat(8), stat(8), stat(HEAD_DIM + V_PAD)],
        compiler_params=_params(("parallel", "parallel"), 48),
        name="fox",
    )(q, k, v, cs, ct4)


def _out_proj_kernel(h_ref, o_ref, w_ref, out_ref):
    o = jnp.concatenate([o_ref[0, p] for p in range(o_ref.shape[1])], axis=1)
    out_ref[0] = h_ref[0] + _dot(o, w_ref[...])


def _out_proj(h, o, w_o, tm=512):
    b, s, d = h.shape
    n_pairs = o.shape[1]
    return pl.pallas_call(
        _out_proj_kernel,
        grid=(b, s // tm),
        in_specs=[pl.BlockSpec((1, tm, d), lambda bi, i: (bi, i, 0)),
                  pl.BlockSpec((1, n_pairs, tm, LANES), lambda bi, i: (bi, 0, i, 0)), _resident(w_o.shape)],
        out_specs=pl.BlockSpec((1, tm, d), lambda bi, i: (bi, i, 0)),
        out_shape=jax.ShapeDtypeStruct((b, s, d), F32),
        compiler_params=_params(("parallel", "parallel"), 32),
        name="out_proj",
    )(h, o, w_o)


def kernel(x, positions, ffn_norm, ffn_w_in, ffn_w_out, mix_norm, a_w_qkv, a_q_norm, a_k_norm, a_w_o,
           kv_norm, kv_w, kv_b_f, kv_k_norm, b_w_q, b_q_norm, b_w_o):
    b, s, d = x.shape
    depth, n_a = ffn_norm.shape[0], a_w_qkv.shape[0]
    hd = a_w_o.shape[1]
    nh = hd // HEAD_DIM
    a_tile = 256

    head_of = np.arange(MXU_DIM) // HEAD_DIM
    bd = jnp.asarray((head_of[:, None] == head_of[None, :]).astype(np.float32) / HEAD_DIM, BF16)
    inv_freq = ROPE_THETA ** (-jnp.arange(0, ROT_DIM, 2, dtype=F32) / ROT_DIM)
    f_head = jnp.concatenate([inv_freq, inv_freq, jnp.zeros((HEAD_DIM - ROT_DIM,), F32)])
    f_row = jnp.tile(f_head, LANES // HEAD_DIM).reshape(1, LANES)
    tile_gain = lambda gain: jnp.tile(gain, nh).reshape(1, hd)
    expand = jnp.asarray((np.arange(LANES)[:, None] == np.arange(hd)[None, :] // HEAD_DIM).astype(np.float32), BF16)

    ffn = lambda h, layer, j: _ffn(h.reshape(b * s, d), ffn_norm[layer, j], ffn_w_in[layer, j].astype(BF16),
                                   ffn_w_out[layer, j].astype(BF16)).reshape(b, s, d)

    h = x
    k_sh = v_sh = cq = ck_rows = None
    for layer in range(depth):
        if layer == n_a:
            k_sh, v_sh, cq, ck_rows = _shared_kv(h, kv_norm, kv_w, kv_b_f, tile_gain(kv_k_norm), bd, expand)
        h = ffn(h, layer, 0)
        if layer < n_a:
            gq = jnp.stack([tile_gain(a_q_norm[layer, g]) for g in range(len(DILATED_GROUPS))])
            gk = jnp.stack([tile_gain(a_k_norm[layer, g]) for g in range(len(DILATED_GROUPS))])
            qkv = _a_qkv(h, mix_norm[layer], a_w_qkv[layer].astype(BF16), positions, f_row, gq, gk, bd, a_tile)
            attn = [_a_attn(t, hd) for t in qkv]
            h = _a_mix(h, [o for o, _ in attn], [l for _, l in attn], a_w_o[layer].astype(BF16), expand, a_tile)
        else:
            j = layer - n_a
            q = _b_q(h, mix_norm[layer], b_w_q[j].astype(BF16), tile_gain(b_q_norm[j]), bd)
            o = _fox(q, k_sh, v_sh, cq, ck_rows)
            h = _out_proj(h, o, b_w_o[j].astype(BF16))
        h = ffn(h, layer, 1)
    return h
```

```python
import functools

import numpy as np
import jax
import jax.numpy as jnp
from jax import lax
from jax.experimental import pallas as pl
from jax.experimental.pallas import tpu as pltpu

F32 = jnp.float32
BF16 = jnp.bfloat16

HEAD_DIM = 64
DILATED_GROUPS = ((128, 1), (512, 4), (2048, 16))
BAND = 128
ROT_DIM = HEAD_DIM // 4
ROPE_THETA = 500000.0
EPS = 1e-6
LANES = 128
MXU_DIM = 256
NEG = -1e30
LOG2E = 1.4426950408889634
LN2 = 0.6931471805599453
V_PAD = 16
MIB = 1024 * 1024


def _params(semantics, vmem_mib):
    return pltpu.CompilerParams(dimension_semantics=semantics, vmem_limit_bytes=vmem_mib * MIB)


def _resident(shape):
    nd = len(shape)
    return pl.BlockSpec(shape, lambda *_: (0,) * nd, pipeline_mode=pl.Buffered(1))


def _dot(a, b):
    return jnp.dot(a, b, preferred_element_type=F32)


def _dot_nt(a, b):
    return lax.dot_general(a, b, (((1,), (1,)), ((), ())), preferred_element_type=F32)


def _rms(x, g):
    ms = jnp.mean(x * x, axis=-1, keepdims=True)
    return x * lax.rsqrt(ms + EPS) * g


def _head_rms(y, bd_ref, gain):
    outs = []
    for c in range(y.shape[1] // MXU_DIM):
        yc = y[:, c * MXU_DIM:(c + 1) * MXU_DIM]
        ms = _dot((yc * yc).astype(BF16), bd_ref[...])
        outs.append(yc * lax.rsqrt(ms + EPS))
    return jnp.concatenate(outs, axis=1) * gain


def _rope_tables(pos_col, f_row):
    ang = pos_col.astype(F32) * f_row
    j = lax.broadcasted_iota(jnp.int32, (1, LANES), 1) % HEAD_DIM
    c, s = jnp.cos(ang), jnp.sin(ang)
    half = ROT_DIM // 2
    s_up = jnp.where(j < half, -s, 0.0)
    s_dn = jnp.where((j >= half) & (j < ROT_DIM), s, 0.0)
    return c, s_up, s_dn


def _rope(y, tabs):
    c, s_up, s_dn = tabs
    half = ROT_DIM // 2
    outs = []
    for k in range(y.shape[1] // LANES):
        yc = y[:, k * LANES:(k + 1) * LANES]
        outs.append(yc * c + pltpu.roll(yc, LANES - half, 1) * s_up + pltpu.roll(yc, half, 1) * s_dn)
    return jnp.concatenate(outs, axis=1)


def _split_bf16(v, parts):
    out, r = [], v
    for _ in range(parts):
        p = r.astype(BF16)
        out.append(p)
        r = r - p.astype(F32)
    return out


def _ffn_kernel(x_ref, g_ref, win_ref, wout_ref, o_ref, *, d_ff, bounds):
    x = x_ref[...]
    xn = _rms(x, g_ref[...]).astype(BF16)
    acc = jnp.zeros_like(x)
    for lo, hi in zip(bounds[:-1], bounds[1:]):
        gate = _dot(xn, win_ref[:, lo:hi])
        up = _dot(xn, win_ref[:, d_ff + lo:d_ff + hi])
        mid = (gate * jax.nn.sigmoid(gate) * up).astype(BF16)
        acc = acc + _dot(mid, wout_ref[lo:hi, :])
    o_ref[...] = x + 0.5 * acc


def _ffn(h2, g, w_in, w_out, tm=512):
    n, d = h2.shape
    d_ff = w_out.shape[0]
    tiles = pl.cdiv(d_ff, MXU_DIM)
    bounds = (0, min(d_ff, pl.cdiv(tiles, 2) * MXU_DIM), d_ff) if tiles > 1 else (0, d_ff)
    return pl.pallas_call(
        functools.partial(_ffn_kernel, d_ff=d_ff, bounds=bounds),
        grid=(n // tm,),
        in_specs=[pl.BlockSpec((tm, d), lambda i: (i, 0)), _resident((1, d)),
                  _resident(w_in.shape), _resident(w_out.shape)],
        out_specs=pl.BlockSpec((tm, d), lambda i: (i, 0)),
        out_shape=jax.ShapeDtypeStruct((n, d), F32),
        compiler_params=_params(("parallel",), 52),
        name="ffn",
    )(h2, g.reshape(1, d), w_in, w_out)


def _perm_matrix(tm, dil):
    p = np.arange(tm)
    t = (p % (tm // dil)) * dil + p // (tm // dil)
    m = np.zeros((tm, tm), np.float32)
    m[p, t] = 1.0
    return m


def _a_qkv_kernel(x_ref, g_ref, w_ref, p4_ref, p16_ref, pos1_ref, pos4_ref, pos16_ref, f_ref, gq_ref, gk_ref,
                  bd_ref, o1_ref, o4_ref, o16_ref):
    hd = gq_ref.shape[-1]
    xn = _rms(x_ref[0], g_ref[...]).astype(BF16)
    per_group = ((None, pos1_ref, o1_ref), (p4_ref, pos4_ref, o4_ref), (p16_ref, pos16_ref, o16_ref))
    for g, (p_ref, pos_ref, o_ref) in enumerate(per_group):
        xp = xn if p_ref is None else _dot(p_ref[...], xn).astype(BF16)
        tabs = _rope_tables(pos_ref[0], f_ref[...])
        dil, rows = o_ref.shape[1], o_ref.shape[2]
        w0 = g * 3 * hd
        q = _dot(xp, w_ref[:, w0:w0 + hd])
        q = _rope(_head_rms(q, bd_ref, gq_ref[g]), tabs) * (HEAD_DIM ** -0.5 * LOG2E)
        o_ref[0, :, :, 0:hd] = q.astype(BF16).reshape(dil, rows, hd)
        k = _dot(xp, w_ref[:, w0 + hd:w0 + 2 * hd])
        k = _rope(_head_rms(k, bd_ref, gk_ref[g]), tabs)
        o_ref[0, :, :, hd:2 * hd] = k.astype(BF16).reshape(dil, rows, hd)
        v = _dot(xp, w_ref[:, w0 + 2 * hd:w0 + 3 * hd])
        o_ref[0, :, :, 2 * hd:3 * hd] = v.astype(BF16).reshape(dil, rows, hd)


def _a_qkv(h, g, w_qkv, positions, f_row, gq, gk, bd, tm):
    b, s, d = h.shape
    hd = gq.shape[-1]
    dils = [dil for _, dil in DILATED_GROUPS]
    perms = [jnp.asarray(_perm_matrix(tm, dil), BF16) for dil in dils[1:]]
    pos = [positions.reshape(b, s // tm, tm // dil, dil).transpose(0, 1, 3, 2).reshape(b, s, 1) for dil in dils]
    row_spec = lambda last: pl.BlockSpec((1, tm, last), lambda bi, i: (bi, i, 0))
    return pl.pallas_call(
        _a_qkv_kernel,
        grid=(b, s // tm),
        in_specs=[row_spec(d), _resident((1, d)), _resident(w_qkv.shape), _resident((tm, tm)), _resident((tm, tm)),
                  row_spec(1), row_spec(1), row_spec(1), _resident((1, LANES)), _resident(gq.shape),
                  _resident(gk.shape), _resident(bd.shape)],
        out_specs=[pl.BlockSpec((1, dil, tm // dil, 3 * hd), lambda bi, i: (bi, 0, i, 0)) for dil in dils],
        out_shape=[jax.ShapeDtypeStruct((b, dil, s // dil, 3 * hd), BF16) for dil in dils],
        compiler_params=_params(("parallel", "parallel"), 52),
        name="a_qkv",
    )(h, g.reshape(1, d), w_qkv, perms[0], perms[1], pos[0], pos[1], pos[2], f_row, gq, gk, bd)


def _a_attn_kernel(q_ref, kp_ref, kc_ref, vp_ref, vc_ref, o_ref, ml_ref):
    n = pl.program_id(2)
    n_heads = q_ref.shape[-1] // HEAD_DIM
    lane = lax.broadcasted_iota(jnp.int32, (BAND, LANES), 1)
    qi = lax.broadcasted_iota(jnp.int32, (BAND, 2 * BAND), 0)
    kj = lax.broadcasted_iota(jnp.int32, (BAND, 2 * BAND), 1)
    band = (kj >= qi) & (kj <= qi + BAND)
    lo_half = lane < HEAD_DIM
    for res, j in [(res, j) for res in range(q_ref.shape[1]) for j in range(q_ref.shape[2] // BAND)]:
        rows = slice(j * BAND, (j + 1) * BAND)
        before = slice((j - 1) * BAND, j * BAND)
        valid = band & ((n > 0) | (kj >= BAND)) if j == 0 else band
        mask = jnp.where(valid, 0.0, NEG)
        mask = jnp.concatenate([mask, mask], axis=0)
        ml_tile = jnp.zeros((BAND, LANES), F32)
        for p in range(q_ref.shape[-1] // LANES):
            sl = slice(p * LANES, (p + 1) * LANES)
            q2 = q_ref[0, res, rows, sl]
            k_prev = kp_ref[0, res, :, sl] if j == 0 else kc_ref[0, res, before, sl]
            v_prev = vp_ref[0, res, :, sl] if j == 0 else vc_ref[0, res, before, sl]
            kk = jnp.concatenate([k_prev, kc_ref[0, res, rows, sl]], axis=0)
            vv = jnp.concatenate([v_prev, vc_ref[0, res, rows, sl]], axis=0)
            zero = jnp.zeros_like(q2)
            qq = jnp.concatenate([jnp.where(lo_half, q2, zero), jnp.where(lo_half, zero, q2)], axis=0)
            sc = _dot_nt(qq, kk) + mask
            m = jnp.max(sc, axis=-1, keepdims=True)
            e = jnp.exp2(sc - m)
            l = jnp.sum(e, axis=-1, keepdims=True)
            o = _dot(e.astype(BF16), vv)
            o_ref[0, res, rows, sl] = jnp.where(lo_half, o[:BAND], o[BAND:]).astype(BF16)
            for half in range(2):
                hrows = slice(half * BAND, (half + 1) * BAND)
                head = 2 * p + half
                ml_tile = jnp.where(lane == head, m[hrows], jnp.where(lane == n_heads + head, l[hrows], ml_tile))
        ml_ref[0, res, rows, :] = ml_tile


def _a_attn(qkv, hd, blocks_per_step=8):
    b, dil, sd, _ = qkv.shape
    seq_blocks = min(blocks_per_step, sd // BAND)
    res = min(dil, blocks_per_step // seq_blocks)
    rows = BAND * seq_blocks
    cur = lambda col: pl.BlockSpec((1, res, rows, hd), lambda bi, r, n: (bi, r, n, col))
    prev = lambda col: pl.BlockSpec(
        (1, res, BAND, hd), lambda bi, r, n: (bi, r, jnp.maximum(n * seq_blocks - 1, 0), col))
    return pl.pallas_call(
        _a_attn_kernel,
        grid=(b, dil // res, sd // rows),
        in_specs=[cur(0), prev(1), cur(1), prev(2), cur(2)],
        out_specs=[pl.BlockSpec((1, res, rows, hd), lambda bi, r, n: (bi, r, n, 0)),
                   pl.BlockSpec((1, res, rows, LANES), lambda bi, r, n: (bi, r, n, 0))],
        out_shape=[jax.ShapeDtypeStruct((b, dil, sd, hd), BF16), jax.ShapeDtypeStruct((b, dil, sd, LANES), F32)],
        compiler_params=_params(("parallel", "parallel", "arbitrary"), 32),
        name=f"a_attn_d{dil}",
    )(qkv, qkv, qkv, qkv, qkv)


def _a_mix_kernel(h_ref, o1_ref, o4_ref, o16_ref, l1_ref, l4_ref, l16_ref, pt4_ref, pt16_ref, e_ref, wo_ref,
                  out_ref):
    tm, hd = o1_ref.shape[2], o1_ref.shape[3]
    n_heads = hd // HEAD_DIM

    def unperm_stats(pt_ref, l_ref):
        parts = _split_bf16(l_ref[0].reshape(tm, LANES), 3)
        return sum(_dot(pt_ref[...], part) for part in parts)

    o = [o1_ref[0, 0].astype(F32),
         _dot(pt4_ref[...], o4_ref[0].reshape(tm, hd)),
         _dot(pt16_ref[...], o16_ref[0].reshape(tm, hd))]
    ml = [l1_ref[0, 0], unperm_stats(pt4_ref, l4_ref), unperm_stats(pt16_ref, l16_ref)]
    m = ml
    l = [pltpu.roll(v, LANES - n_heads, 1) for v in ml]
    m_max = jnp.maximum(jnp.maximum(m[0], m[1]), m[2])
    w = [jnp.exp2(v - m_max) for v in m]
    den = w[0] * l[0] + w[1] * l[1] + w[2] * l[2]
    is_head = lax.broadcasted_iota(jnp.int32, (tm, LANES), 1) < n_heads
    mixed = jnp.zeros((tm, hd), F32)
    for wg, og in zip(w, o):
        coef = jnp.where(is_head, wg / den, 0.0).astype(BF16)
        mixed = mixed + _dot(coef, e_ref[...]) * og
    out_ref[0] = h_ref[0] + _dot(mixed.astype(BF16), wo_ref[...])


def _a_mix(h, outs, lses, w_o, expand, tm):
    b, s, d = h.shape
    hd = w_o.shape[0]
    dils = [dil for _, dil in DILATED_GROUPS]
    pts = [jnp.asarray(_perm_matrix(tm, dil).T, BF16) for dil in dils[1:]]
    perm_spec = lambda dil, last: pl.BlockSpec((1, dil, tm // dil, last), lambda bi, i: (bi, 0, i, 0))
    return pl.pallas_call(
        _a_mix_kernel,
        grid=(b, s // tm),
        in_specs=[pl.BlockSpec((1, tm, d), lambda bi, i: (bi, i, 0))]
        + [perm_spec(dil, hd) for dil in dils] + [perm_spec(dil, LANES) for dil in dils]
        + [_resident((tm, tm)), _resident((tm, tm)), _resident(expand.shape), _resident(w_o.shape)],
        out_specs=pl.BlockSpec((1, tm, d), lambda bi, i: (bi, i, 0)),
        out_shape=jax.ShapeDtypeStruct((b, s, d), F32),
        compiler_params=_params(("parallel", "parallel"), 32),
        name="a_mix",
    )(h, *outs, *lses, pts[0], pts[1], expand, w_o)


def _store_pairs(ref, y):
    for p in range(ref.shape[1]):
        ref[0, p] = y[:, p * LANES:(p + 1) * LANES].astype(ref.dtype)


def _shared_kv_kernel(x_ref, g_ref, wk_ref, wv_ref, wf_ref, bf_ref, gk_ref, bd_ref, tri_ref, e_ref,
                      k_ref, v_ref, cq_ref, ck_ref, carry_ref):
    @pl.when(pl.program_id(1) == 0)
    def _():
        carry_ref[...] = jnp.zeros_like(carry_ref)

    xn = _rms(x_ref[0], g_ref[...]).astype(BF16)
    _store_pairs(k_ref, _head_rms(_dot(xn, wk_ref[...]), bd_ref, gk_ref[...]))
    _store_pairs(v_ref, _dot(xn, wv_ref[...]))
    z = _dot(xn, wf_ref[...]) + bf_ref[...]
    log_f = jnp.minimum(z, 0.0) - jnp.log(1.0 + jnp.exp(-jnp.abs(z)))
    cum = carry_ref[...] + sum(_dot(tri_ref[...], part) for part in _split_bf16(log_f, 3))
    carry_ref[...] = cum[-1:, :]
    cum2 = cum * LOG2E
    ck_ref[0] = cum2.T[:ck_ref.shape[1], :]
    _store_pairs(cq_ref, sum(_dot(part, e_ref[...]) for part in _split_bf16(cum2, 3)))


def _shared_kv(h, g, w_kv, b_f, gk, bd, expand, tm=512):
    b, s, d = h.shape
    hd = gk.shape[-1]
    nh, n_pairs = hd // HEAD_DIM, hd // LANES
    wk, wv = w_kv[:, :hd].astype(BF16), w_kv[:, hd:2 * hd].astype(BF16)
    wf = jnp.pad(w_kv[:, 2 * hd:], ((0, 0), (0, LANES - nh))).astype(BF16)
    bf = jnp.pad(b_f, (0, LANES - nh)).reshape(1, LANES)
    tri = jnp.asarray(np.tril(np.ones((tm, tm), np.float32)), BF16)
    pairs = lambda: pl.BlockSpec((1, n_pairs, tm, LANES), lambda bi, i: (bi, 0, i, 0))
    return pl.pallas_call(
        _shared_kv_kernel,
        grid=(b, s // tm),
        in_specs=[pl.BlockSpec((1, tm, d), lambda bi, i: (bi, i, 0)), _resident((1, d)), _resident(wk.shape),
                  _resident(wv.shape), _resident(wf.shape), _resident(bf.shape), _resident(gk.shape),
                  _resident(bd.shape), _resident(tri.shape), _resident(expand.shape)],
        out_specs=[pairs(), pairs(), pairs(), pl.BlockSpec((1, nh, tm), lambda bi, i: (bi, 0, i))],
        out_shape=[jax.ShapeDtypeStruct((b, n_pairs, s, LANES), BF16), jax.ShapeDtypeStruct((b, n_pairs, s, LANES), BF16),
                   jax.ShapeDtypeStruct((b, n_pairs, s, LANES), F32), jax.ShapeDtypeStruct((b, nh, s), F32)],
        scratch_shapes=[pltpu.VMEM((1, LANES), F32)],
        compiler_params=_params(("parallel", "arbitrary"), 40),
        name="shared_kv",
    )(h, g.reshape(1, d), wk, wv, wf, bf, gk, bd, tri, expand)


def _b_q_kernel(x_ref, g_ref, w_ref, gq_ref, bd_ref, o_ref):
    xn = _rms(x_ref[0], g_ref[...]).astype(BF16)
    q = _head_rms(_dot(xn, w_ref[...]), bd_ref, gq_ref[...]) * (HEAD_DIM ** -0.5 * LOG2E)
    _store_pairs(o_ref, q)


def _b_q(h, g, w_q, gq, bd, tm=512):
    b, s, d = h.shape
    n_pairs = w_q.shape[1] // LANES
    return pl.pallas_call(
        _b_q_kernel,
        grid=(b, s // tm),
        in_specs=[pl.BlockSpec((1, tm, d), lambda bi, i: (bi, i, 0)), _resident((1, d)), _resident(w_q.shape),
                  _resident(gq.shape), _resident(bd.shape)],
        out_specs=pl.BlockSpec((1, n_pairs, tm, LANES), lambda bi, i: (bi, 0, i, 0)),
        out_shape=jax.ShapeDtypeStruct((b, n_pairs, s, LANES), BF16),
        compiler_params=_params(("parallel", "parallel"), 32),
        name="b_q",
    )(h, g.reshape(1, d), w_q, gq, bd)


def _fox_kernel(q_ref, k_ref, v_ref, cs_ref, ct_ref, o_ref, s_scr, vt_scr, cs_scr, mask_scr, mx_scr, mb_scr, acc_scr,
                *, blk):
    n_tiles = q_ref.shape[2] // blk
    nj = blk // LANES
    groups = blk // 8
    lo_half = lax.broadcasted_iota(jnp.int32, (blk, LANES), 1) < HEAD_DIM
    lo_rows = lax.broadcasted_iota(jnp.int32, (LANES, blk), 0) < HEAD_DIM
    key = lax.broadcasted_iota(jnp.int32, (blk, blk), 0)
    qry = lax.broadcasted_iota(jnp.int32, (blk, blk), 1)
    mask_scr[...] = jnp.where(key <= qry, 0.0, NEG)
    ones_rows = jnp.where(lax.broadcasted_iota(jnp.int32, (V_PAD, blk), 0) == 0, 1.0, 0.0).astype(BF16)
    for c in range(n_tiles):
        rows = slice(c * blk, (c + 1) * blk)
        vt = v_ref[0, 0, rows, :].T
        for half in range(2):
            vt_scr[c, half, 0:HEAD_DIM] = vt[half * HEAD_DIM:(half + 1) * HEAD_DIM]
            vt_scr[c, half, HEAD_DIM:HEAD_DIM + V_PAD] = ones_rows
        cs2 = cs_ref[0, 0, rows, :]
        swapped = pltpu.roll(cs2, HEAD_DIM, 1)
        cs_scr[0, rows] = jnp.where(lo_half, cs2, swapped)
        cs_scr[1, rows] = jnp.where(lo_half, swapped, cs2)

    def load_q(i):
        qt = q_ref[0, 0, pl.ds(pl.multiple_of(i * blk, blk), blk), :].T
        zero = jnp.zeros_like(qt)
        qq = jnp.concatenate([jnp.where(lo_rows, qt, zero), jnp.where(lo_rows, zero, qt)], axis=1)
        return qq, (ct_ref[0, 0, pl.ds(i, 1), :], ct_ref[0, 1, pl.ds(i, 1), :])

    def scores(qq, ct, c, diagonal):
        r0 = pl.multiple_of(c * blk, blk)
        st = _dot(k_ref[0, 0, pl.ds(r0, blk), :], qq)
        for half in range(2):
            cs = cs_scr[half, pl.ds(r0, blk), :]
            sc = jnp.concatenate(
                [st[:, half * blk + j * LANES:half * blk + (j + 1) * LANES] - cs for j in range(nj)], axis=1)
            if diagonal:
                sc = sc + mask_scr[...]
            s_scr[half, c] = sc
            mx_scr[half] = jnp.maximum(mx_scr[half], jnp.max(sc.reshape(groups, 8, blk), axis=0))

    def weights(c):
        for half in range(2):
            e = jnp.exp2(s_scr[half, c] - mb_scr[half][0:1, :])
            acc_scr[half] = acc_scr[half] + _dot(vt_scr[c, half], e.astype(BF16))

    def finish_scores(ct):
        for half in range(2):
            row_max = jnp.max(mx_scr[half], axis=0, keepdims=True) + ct[half]
            shift = row_max - ct[half]
            mb_scr[half] = jnp.broadcast_to(shift, (8, blk))
        mx_scr[...] = jnp.full_like(mx_scr, NEG)
        acc_scr[...] = jnp.zeros_like(acc_scr)

    def write_out(i):
        ot = jnp.concatenate([acc_scr[half, 0:HEAD_DIM] * (1.0 / acc_scr[half, HEAD_DIM:HEAD_DIM + 1])
                              for half in range(2)], axis=0)
        o_ref[0, 0, pl.ds(pl.multiple_of(i * blk, blk), blk), :] = ot.T.astype(BF16)

    mx_scr[...] = jnp.full_like(mx_scr, NEG)
    qq0, ct0 = load_q(0)
    scores(qq0, ct0, 0, True)
    finish_scores(ct0)

    def tile(i, carry):
        qq, ct = load_q(i + 1)
        n_full = i + 1

        def both(c):
            weights(c)
            scores(qq, ct, c, False)

        def four_chunks(c4, carry1):
            for k in range(4):
                both(4 * c4 + k)
            return carry1

        lax.fori_loop(0, lax.shift_right_logical(n_full, 2), four_chunks, 0)
        done = n_full & ~3

        @pl.when((n_full & 2) != 0)
        def _():
            both(done)
            both(done + 1)

        @pl.when((n_full & 1) != 0)
        def _():
            both(n_full - 1)

        scores(qq, ct, n_full, True)
        write_out(i)
        finish_scores(ct)
        return carry

    lax.fori_loop(0, n_tiles - 1, tile, 0)

    def last(c, carry1):
        weights(c)
        return carry1

    lax.fori_loop(0, n_tiles, last, 0, unroll=4)
    write_out(n_tiles - 1)


def _fox(q, k, v, cs, ct_rows, blk=512):
    b, n_pairs, s, _ = q.shape
    nh = ct_rows.shape[1]
    ct4 = ct_rows.reshape(b, nh, s // blk, blk)
    pair = lambda: pl.BlockSpec((1, 1, s, LANES), lambda bi, p: (bi, p, 0, 0))
    stat = lambda rows: pltpu.VMEM((2, rows, blk), F32)
    return pl.pallas_call(
        functools.partial(_fox_kernel, blk=blk),
        grid=(b, n_pairs),
        in_specs=[pair(), pair(), pair(), pair(), pl.BlockSpec((1, 2, s // blk, blk), lambda bi, p: (bi, p, 0, 0))],
        out_specs=pair(),
        out_shape=jax.ShapeDtypeStruct((b, n_pairs, s, LANES), BF16),
        scratch_shapes=[pltpu.VMEM((2, s // blk, blk, blk), F32),
                        pltpu.VMEM((s // blk, 2, HEAD_DIM + V_PAD, blk), BF16),
                        pltpu.VMEM((2, s, LANES), F32), pltpu.VMEM((blk, blk), F32),
                        stat(8), stat(8), stat(HEAD_DIM + V_PAD)],
        compiler_params=_params(("parallel", "parallel"), 48),
        name="fox",
    )(q, k, v, cs, ct4)


def _out_proj_kernel(h_ref, o_ref, w_ref, out_ref):
    o = jnp.concatenate([o_ref[0, p] for p in range(o_ref.shape[1])], axis=1)
    out_ref[0] = h_ref[0] + _dot(o, w_ref[...])


def _out_proj(h, o, w_o, tm=512):
    b, s, d = h.shape
    n_pairs = o.shape[1]
    return pl.pallas_call(
        _out_proj_kernel,
        grid=(b, s // tm),
        in_specs=[pl.BlockSpec((1, tm, d), lambda bi, i: (bi, i, 0)),
                  pl.BlockSpec((1, n_pairs, tm, LANES), lambda bi, i: (bi, 0, i, 0)), _resident(w_o.shape)],
        out_specs=pl.BlockSpec((1, tm, d), lambda bi, i: (bi, i, 0)),
        out_shape=jax.ShapeDtypeStruct((b, s, d), F32),
        compiler_params=_params(("parallel", "parallel"), 32),
        name="out_proj",
    )(h, o, w_o)


def kernel(x, positions, ffn_norm, ffn_w_in, ffn_w_out, mix_norm, a_w_qkv, a_q_norm, a_k_norm, a_w_o,
           kv_norm, kv_w, kv_b_f, kv_k_norm, b_w_q, b_q_norm, b_w_o):
    b, s, d = x.shape
    depth, n_a = ffn_norm.shape[0], a_w_qkv.shape[0]
    hd = a_w_o.shape[1]
    nh = hd // HEAD_DIM
    a_tile = 256

    head_of = np.arange(MXU_DIM) // HEAD_DIM
    bd = jnp.asarray((head_of[:, None] == head_of[None, :]).astype(np.float32) / HEAD_DIM, BF16)
    inv_freq = ROPE_THETA ** (-jnp.arange(0, ROT_DIM, 2, dtype=F32) / ROT_DIM)
    f_head = jnp.concatenate([inv_freq, inv_freq, jnp.zeros((HEAD_DIM - ROT_DIM,), F32)])
    f_row = jnp.tile(f_head, LANES // HEAD_DIM).reshape(1, LANES)
    tile_gain = lambda gain: jnp.tile(gain, nh).reshape(1, hd)
    expand = jnp.asarray((np.arange(LANES)[:, None] == np.arange(hd)[None, :] // HEAD_DIM).astype(np.float32), BF16)

    ffn = lambda h, layer, j: _ffn(h.reshape(b * s, d), ffn_norm[layer, j], ffn_w_in[layer, j].astype(BF16),
                                   ffn_w_out[layer, j].astype(BF16)).reshape(b, s, d)

    h = x
    k_sh = v_sh = cq = ck_rows = None
    for layer in range(depth):
        if layer == n_a:
            k_sh, v_sh, cq, ck_rows = _shared_kv(h, kv_norm, kv_w, kv_b_f, tile_gain(kv_k_norm), bd, expand)
        h = ffn(h, layer, 0)
        if layer < n_a:
            gq = jnp.stack([tile_gain(a_q_norm[layer, g]) for g in range(len(DILATED_GROUPS))])
            gk = jnp.stack([tile_gain(a_k_norm[layer, g]) for g in range(len(DILATED_GROUPS))])
            qkv = _a_qkv(h, mix_norm[layer], a_w_qkv[layer].astype(BF16), positions, f_row, gq, gk, bd, a_tile)
            attn = [_a_attn(t, hd) for t in qkv]
            h = _a_mix(h, [o for o, _ in attn], [l for _, l in attn], a_w_o[layer].astype(BF16), expand, a_tile)
        else:
            j = layer - n_a
            q = _b_q(h, mix_norm[layer], b_w_q[j].astype(BF16), tile_gain(b_q_norm[j]), bd)
            o = _fox(q, k_sh, v_sh, cq, ck_rows)
            h = _out_proj(h, o, b_w_o[j].astype(BF16))
        h = ffn(h, layer, 1)
    return h
```

```python
import functools

import numpy as np
import jax
import jax.numpy as jnp
from jax import lax
from jax.experimental import pallas as pl
from jax.experimental.pallas import tpu as pltpu

F32 = jnp.float32
BF16 = jnp.bfloat16

HEAD_DIM = 64
DILATED_GROUPS = ((128, 1), (512, 4), (2048, 16))
BAND = 128
ROT_DIM = HEAD_DIM // 4
ROPE_THETA = 500000.0
EPS = 1e-6
LANES = 128
MXU_DIM = 256
NEG = -1e30
LOG2E = 1.4426950408889634
LN2 = 0.6931471805599453
SUB_TILE = 512
V_PAD = 16
MIB = 1024 * 1024


def _params(semantics, vmem_mib):
    return pltpu.CompilerParams(dimension_semantics=semantics, vmem_limit_bytes=vmem_mib * MIB)


def _resident(shape):
    nd = len(shape)
    return pl.BlockSpec(shape, lambda *_: (0,) * nd, pipeline_mode=pl.Buffered(1))


def _dot(a, b):
    return jnp.dot(a, b, preferred_element_type=F32)


def _dot_nt(a, b):
    return lax.dot_general(a, b, (((1,), (1,)), ((), ())), preferred_element_type=F32)


def _rms(x, g):
    ms = jnp.mean(x * x, axis=-1, keepdims=True)
    return x * lax.rsqrt(ms + EPS) * g


def _head_rms(y, bd_ref, gain):
    outs = []
    for c in range(y.shape[1] // MXU_DIM):
        yc = y[:, c * MXU_DIM:(c + 1) * MXU_DIM]
        ms = _dot((yc * yc).astype(BF16), bd_ref[...])
        outs.append(yc * lax.rsqrt(ms + EPS))
    return jnp.concatenate(outs, axis=1) * gain


def _rope_tables(pos_col, f_row):
    ang = pos_col.astype(F32) * f_row
    j = lax.broadcasted_iota(jnp.int32, (1, LANES), 1) % HEAD_DIM
    c, s = jnp.cos(ang), jnp.sin(ang)
    half = ROT_DIM // 2
    s_up = jnp.where(j < half, -s, 0.0)
    s_dn = jnp.where((j >= half) & (j < ROT_DIM), s, 0.0)
    return c, s_up, s_dn


def _rope(y, tabs):
    c, s_up, s_dn = tabs
    half = ROT_DIM // 2
    outs = []
    for k in range(y.shape[1] // LANES):
        yc = y[:, k * LANES:(k + 1) * LANES]
        outs.append(yc * c + pltpu.roll(yc, LANES - half, 1) * s_up + pltpu.roll(yc, half, 1) * s_dn)
    return jnp.concatenate(outs, axis=1)


def _split_bf16(v, parts):
    out, r = [], v
    for _ in range(parts):
        p = r.astype(BF16)
        out.append(p)
        r = r - p.astype(F32)
    return out


def _ffn_kernel(x_ref, g_ref, win_ref, wout_ref, o_ref, *, d_ff, bounds):
    x = x_ref[...]
    xn = _rms(x, g_ref[...]).astype(BF16)
    acc = jnp.zeros_like(x)
    for lo, hi in zip(bounds[:-1], bounds[1:]):
        gate = _dot(xn, win_ref[:, lo:hi])
        up = _dot(xn, win_ref[:, d_ff + lo:d_ff + hi])
        mid = (gate * jax.nn.sigmoid(gate) * up).astype(BF16)
        acc = acc + _dot(mid, wout_ref[lo:hi, :])
    o_ref[...] = x + 0.5 * acc


def _ffn(h2, g, w_in, w_out, tm=512):
    n, d = h2.shape
    d_ff = w_out.shape[0]
    tiles = pl.cdiv(d_ff, MXU_DIM)
    bounds = (0, min(d_ff, pl.cdiv(tiles, 2) * MXU_DIM), d_ff) if tiles > 1 else (0, d_ff)
    return pl.pallas_call(
        functools.partial(_ffn_kernel, d_ff=d_ff, bounds=bounds),
        grid=(n // tm,),
        in_specs=[pl.BlockSpec((tm, d), lambda i: (i, 0)), _resident((1, d)),
                  _resident(w_in.shape), _resident(w_out.shape)],
        out_specs=pl.BlockSpec((tm, d), lambda i: (i, 0)),
        out_shape=jax.ShapeDtypeStruct((n, d), F32),
        compiler_params=_params(("parallel",), 52),
        name="ffn",
    )(h2, g.reshape(1, d), w_in, w_out)


def _perm_matrix(tm, dil):
    p = np.arange(tm)
    t = (p % (tm // dil)) * dil + p // (tm // dil)
    m = np.zeros((tm, tm), np.float32)
    m[p, t] = 1.0
    return m


def _a_qkv_kernel(x_ref, g_ref, w_ref, p4_ref, p16_ref, pos1_ref, pos4_ref, pos16_ref, f_ref, gq_ref, gk_ref,
                  bd_ref, o1_ref, o4_ref, o16_ref):
    hd = gq_ref.shape[-1]
    xn = _rms(x_ref[0], g_ref[...]).astype(BF16)
    per_group = ((None, pos1_ref, o1_ref), (p4_ref, pos4_ref, o4_ref), (p16_ref, pos16_ref, o16_ref))
    for g, (p_ref, pos_ref, o_ref) in enumerate(per_group):
        xp = xn if p_ref is None else _dot(p_ref[...], xn).astype(BF16)
        tabs = _rope_tables(pos_ref[0], f_ref[...])
        dil, rows = o_ref.shape[1], o_ref.shape[2]
        w0 = g * 3 * hd
        q = _dot(xp, w_ref[:, w0:w0 + hd])
        q = _rope(_head_rms(q, bd_ref, gq_ref[g]), tabs) * (HEAD_DIM ** -0.5 * LOG2E)
        o_ref[0, :, :, 0:hd] = q.astype(BF16).reshape(dil, rows, hd)
        k = _dot(xp, w_ref[:, w0 + hd:w0 + 2 * hd])
        k = _rope(_head_rms(k, bd_ref, gk_ref[g]), tabs)
        o_ref[0, :, :, hd:2 * hd] = k.astype(BF16).reshape(dil, rows, hd)
        v = _dot(xp, w_ref[:, w0 + 2 * hd:w0 + 3 * hd])
        o_ref[0, :, :, 2 * hd:3 * hd] = v.astype(BF16).reshape(dil, rows, hd)


def _a_qkv(h, g, w_qkv, positions, f_row, gq, gk, bd, tm):
    b, s, d = h.shape
    hd = gq.shape[-1]
    dils = [dil for _, dil in DILATED_GROUPS]
    perms = [jnp.asarray(_perm_matrix(tm, dil), BF16) for dil in dils[1:]]
    pos = [positions.reshape(b, s // tm, tm // dil, dil).transpose(0, 1, 3, 2).reshape(b, s, 1) for dil in dils]
    row_spec = lambda last: pl.BlockSpec((1, tm, last), lambda bi, i: (bi, i, 0))
    return pl.pallas_call(
        _a_qkv_kernel,
        grid=(b, s // tm),
        in_specs=[row_spec(d), _resident((1, d)), _resident(w_qkv.shape), _resident((tm, tm)), _resident((tm, tm)),
                  row_spec(1), row_spec(1), row_spec(1), _resident((1, LANES)), _resident(gq.shape),
                  _resident(gk.shape), _resident(bd.shape)],
        out_specs=[pl.BlockSpec((1, dil, tm // dil, 3 * hd), lambda bi, i: (bi, 0, i, 0)) for dil in dils],
        out_shape=[jax.ShapeDtypeStruct((b, dil, s // dil, 3 * hd), BF16) for dil in dils],
        compiler_params=_params(("parallel", "parallel"), 52),
        name="a_qkv",
    )(h, g.reshape(1, d), w_qkv, perms[0], perms[1], pos[0], pos[1], pos[2], f_row, gq, gk, bd)


def _a_attn_kernel(q_ref, kp_ref, kc_ref, vp_ref, vc_ref, o_ref, ml_ref):
    n = pl.program_id(2)
    n_heads = q_ref.shape[-1] // HEAD_DIM
    lane = lax.broadcasted_iota(jnp.int32, (BAND, LANES), 1)
    qi = lax.broadcasted_iota(jnp.int32, (BAND, 2 * BAND), 0)
    kj = lax.broadcasted_iota(jnp.int32, (BAND, 2 * BAND), 1)
    band = (kj >= qi) & (kj <= qi + BAND)
    lo_half = lane < HEAD_DIM
    for res, j in [(res, j) for res in range(q_ref.shape[1]) for j in range(q_ref.shape[2] // BAND)]:
        rows = slice(j * BAND, (j + 1) * BAND)
        before = slice((j - 1) * BAND, j * BAND)
        valid = band & ((n > 0) | (kj >= BAND)) if j == 0 else band
        mask = jnp.where(valid, 0.0, NEG)
        mask = jnp.concatenate([mask, mask], axis=0)
        ml_tile = jnp.zeros((BAND, LANES), F32)
        for p in range(q_ref.shape[-1] // LANES):
            sl = slice(p * LANES, (p + 1) * LANES)
            q2 = q_ref[0, res, rows, sl]
            k_prev = kp_ref[0, res, :, sl] if j == 0 else kc_ref[0, res, before, sl]
            v_prev = vp_ref[0, res, :, sl] if j == 0 else vc_ref[0, res, before, sl]
            kk = jnp.concatenate([k_prev, kc_ref[0, res, rows, sl]], axis=0)
            vv = jnp.concatenate([v_prev, vc_ref[0, res, rows, sl]], axis=0)
            zero = jnp.zeros_like(q2)
            qq = jnp.concatenate([jnp.where(lo_half, q2, zero), jnp.where(lo_half, zero, q2)], axis=0)
            sc = _dot_nt(qq, kk) + mask
            m = jnp.max(sc, axis=-1, keepdims=True)
            e = jnp.exp2(sc - m)
            l = jnp.sum(e, axis=-1, keepdims=True)
            o = _dot(e.astype(BF16), vv)
            o_ref[0, res, rows, sl] = jnp.where(lo_half, o[:BAND], o[BAND:]).astype(BF16)
            for half in range(2):
                hrows = slice(half * BAND, (half + 1) * BAND)
                head = 2 * p + half
                ml_tile = jnp.where(lane == head, m[hrows], jnp.where(lane == n_heads + head, l[hrows], ml_tile))
        ml_ref[0, res, rows, :] = ml_tile


def _a_attn(qkv, hd, blocks_per_step=8):
    b, dil, sd, _ = qkv.shape
    seq_blocks = min(blocks_per_step, sd // BAND)
    res = min(dil, blocks_per_step // seq_blocks)
    rows = BAND * seq_blocks
    cur = lambda col: pl.BlockSpec((1, res, rows, hd), lambda bi, r, n: (bi, r, n, col))
    prev = lambda col: pl.BlockSpec(
        (1, res, BAND, hd), lambda bi, r, n: (bi, r, jnp.maximum(n * seq_blocks - 1, 0), col))
    return pl.pallas_call(
        _a_attn_kernel,
        grid=(b, dil // res, sd // rows),
        in_specs=[cur(0), prev(1), cur(1), prev(2), cur(2)],
        out_specs=[pl.BlockSpec((1, res, rows, hd), lambda bi, r, n: (bi, r, n, 0)),
                   pl.BlockSpec((1, res, rows, LANES), lambda bi, r, n: (bi, r, n, 0))],
        out_shape=[jax.ShapeDtypeStruct((b, dil, sd, hd), BF16), jax.ShapeDtypeStruct((b, dil, sd, LANES), F32)],
        compiler_params=_params(("parallel", "parallel", "arbitrary"), 32),
        name=f"a_attn_d{dil}",
    )(qkv, qkv, qkv, qkv, qkv)


def _a_mix_kernel(h_ref, o1_ref, o4_ref, o16_ref, l1_ref, l4_ref, l16_ref, pt4_ref, pt16_ref, e_ref, wo_ref,
                  out_ref):
    tm, hd = o1_ref.shape[2], o1_ref.shape[3]
    n_heads = hd // HEAD_DIM

    def unperm_stats(pt_ref, l_ref):
        parts = _split_bf16(l_ref[0].reshape(tm, LANES), 3)
        return sum(_dot(pt_ref[...], part) for part in parts)

    o = [o1_ref[0, 0].astype(F32),
         _dot(pt4_ref[...], o4_ref[0].reshape(tm, hd)),
         _dot(pt16_ref[...], o16_ref[0].reshape(tm, hd))]
    ml = [l1_ref[0, 0], unperm_stats(pt4_ref, l4_ref), unperm_stats(pt16_ref, l16_ref)]
    m = ml
    l = [pltpu.roll(v, LANES - n_heads, 1) for v in ml]
    m_max = jnp.maximum(jnp.maximum(m[0], m[1]), m[2])
    w = [jnp.exp2(v - m_max) for v in m]
    den = w[0] * l[0] + w[1] * l[1] + w[2] * l[2]
    is_head = lax.broadcasted_iota(jnp.int32, (tm, LANES), 1) < n_heads
    mixed = jnp.zeros((tm, hd), F32)
    for wg, og in zip(w, o):
        coef = jnp.where(is_head, wg / den, 0.0).astype(BF16)
        mixed = mixed + _dot(coef, e_ref[...]) * og
    out_ref[0] = h_ref[0] + _dot(mixed.astype(BF16), wo_ref[...])


def _a_mix(h, outs, lses, w_o, expand, tm):
    b, s, d = h.shape
    hd = w_o.shape[0]
    dils = [dil for _, dil in DILATED_GROUPS]
    pts = [jnp.asarray(_perm_matrix(tm, dil).T, BF16) for dil in dils[1:]]
    perm_spec = lambda dil, last: pl.BlockSpec((1, dil, tm // dil, last), lambda bi, i: (bi, 0, i, 0))
    return pl.pallas_call(
        _a_mix_kernel,
        grid=(b, s // tm),
        in_specs=[pl.BlockSpec((1, tm, d), lambda bi, i: (bi, i, 0))]
        + [perm_spec(dil, hd) for dil in dils] + [perm_spec(dil, LANES) for dil in dils]
        + [_resident((tm, tm)), _resident((tm, tm)), _resident(expand.shape), _resident(w_o.shape)],
        out_specs=pl.BlockSpec((1, tm, d), lambda bi, i: (bi, i, 0)),
        out_shape=jax.ShapeDtypeStruct((b, s, d), F32),
        compiler_params=_params(("parallel", "parallel"), 32),
        name="a_mix",
    )(h, *outs, *lses, pts[0], pts[1], expand, w_o)


def _store_pairs(ref, y, rows=slice(None)):
    for p in range(ref.shape[1]):
        ref[0, p, rows] = y[:, p * LANES:(p + 1) * LANES].astype(ref.dtype)


def _sub_tiles(n_rows):
    return [slice(r, r + SUB_TILE) for r in range(0, n_rows, SUB_TILE)]


def _shared_kv_kernel(x_ref, g_ref, wk_ref, wv_ref, wf_ref, bf_ref, gk_ref, bd_ref, tri_ref,
                      k_ref, v_ref, ck_ref, carry_ref):
    @pl.when(pl.program_id(1) == 0)
    def _():
        carry_ref[...] = jnp.zeros_like(carry_ref)

    for rows in _sub_tiles(x_ref.shape[1]):
        xn = _rms(x_ref[0, rows], g_ref[...]).astype(BF16)
        _store_pairs(k_ref, _head_rms(_dot(xn, wk_ref[...]), bd_ref, gk_ref[...]), rows)
        _store_pairs(v_ref, _dot(xn, wv_ref[...]), rows)
        z = _dot(xn, wf_ref[...]) + bf_ref[...]
        log_f = jnp.minimum(z, 0.0) - jnp.log(1.0 + jnp.exp(-jnp.abs(z)))
        cum = carry_ref[...] + sum(_dot(tri_ref[...], part) for part in _split_bf16(log_f, 3))
        carry_ref[...] = cum[-1:, :]
        ck_ref[0, :, rows] = (cum * LOG2E).T[:ck_ref.shape[1], :]


def _shared_kv(h, g, w_kv, b_f, gk, bd, tm=2 * SUB_TILE):
    b, s, d = h.shape
    hd = gk.shape[-1]
    nh, n_pairs = hd // HEAD_DIM, hd // LANES
    wk, wv = w_kv[:, :hd].astype(BF16), w_kv[:, hd:2 * hd].astype(BF16)
    wf = jnp.pad(w_kv[:, 2 * hd:], ((0, 0), (0, LANES - nh))).astype(BF16)
    bf = jnp.pad(b_f, (0, LANES - nh)).reshape(1, LANES)
    tri = jnp.asarray(np.tril(np.ones((SUB_TILE, SUB_TILE), np.float32)), BF16)
    pairs = lambda: pl.BlockSpec((1, n_pairs, tm, LANES), lambda bi, i: (bi, 0, i, 0))
    return pl.pallas_call(
        _shared_kv_kernel,
        grid=(b, s // tm),
        in_specs=[pl.BlockSpec((1, tm, d), lambda bi, i: (bi, i, 0)), _resident((1, d)), _resident(wk.shape),
                  _resident(wv.shape), _resident(wf.shape), _resident(bf.shape), _resident(gk.shape),
                  _resident(bd.shape), _resident(tri.shape)],
        out_specs=[pairs(), pairs(), pl.BlockSpec((1, nh, tm), lambda bi, i: (bi, 0, i))],
        out_shape=[jax.ShapeDtypeStruct((b, n_pairs, s, LANES), BF16), jax.ShapeDtypeStruct((b, n_pairs, s, LANES), BF16),
                   jax.ShapeDtypeStruct((b, nh, s), F32)],
        scratch_shapes=[pltpu.VMEM((1, LANES), F32)],
        compiler_params=_params(("parallel", "arbitrary"), 40),
        name="shared_kv",
    )(h, g.reshape(1, d), wk, wv, wf, bf, gk, bd, tri)


def _b_q_kernel(x_ref, g_ref, w_ref, gq_ref, bd_ref, o_ref):
    for rows in _sub_tiles(x_ref.shape[1]):
        xn = _rms(x_ref[0, rows], g_ref[...]).astype(BF16)
        q = _head_rms(_dot(xn, w_ref[...]), bd_ref, gq_ref[...]) * (HEAD_DIM ** -0.5 * LOG2E)
        _store_pairs(o_ref, q, rows)


def _b_q(h, g, w_q, gq, bd, tm=2 * SUB_TILE):
    b, s, d = h.shape
    n_pairs = w_q.shape[1] // LANES
    return pl.pallas_call(
        _b_q_kernel,
        grid=(b, s // tm),
        in_specs=[pl.BlockSpec((1, tm, d), lambda bi, i: (bi, i, 0)), _resident((1, d)), _resident(w_q.shape),
                  _resident(gq.shape), _resident(bd.shape)],
        out_specs=pl.BlockSpec((1, n_pairs, tm, LANES), lambda bi, i: (bi, 0, i, 0)),
        out_shape=jax.ShapeDtypeStruct((b, n_pairs, s, LANES), BF16),
        compiler_params=_params(("parallel", "parallel"), 32),
        name="b_q",
    )(h, g.reshape(1, d), w_q, gq, bd)


def _fox_kernel(q_ref, k_ref, v_ref, ct_ref, o_ref, s_scr, vt_scr, cs_scr, mask_scr, mx_scr, mb_scr, acc_scr, *, blk):
    n_tiles = q_ref.shape[2] // blk
    nj = blk // LANES
    groups = blk // 8
    lo_rows = lax.broadcasted_iota(jnp.int32, (LANES, blk), 0) < HEAD_DIM
    key = lax.broadcasted_iota(jnp.int32, (blk, blk), 0)
    qry = lax.broadcasted_iota(jnp.int32, (blk, blk), 1)
    mask_scr[...] = jnp.where(key <= qry, 0.0, NEG)
    ones_rows = jnp.where(lax.broadcasted_iota(jnp.int32, (V_PAD, blk), 0) == 0, 1.0, 0.0).astype(BF16)
    for c in range(n_tiles):
        rows = slice(c * blk, (c + 1) * blk)
        vt = v_ref[0, 0, rows, :].T
        for half in range(2):
            vt_scr[c, half, 0:HEAD_DIM] = vt[half * HEAD_DIM:(half + 1) * HEAD_DIM]
            vt_scr[c, half, HEAD_DIM:HEAD_DIM + V_PAD] = ones_rows
            cs_scr[half, rows] = jnp.broadcast_to(ct_ref[0, half, c:c + 1, :], (LANES, blk)).T

    def load_q(i):
        qt = q_ref[0, 0, pl.ds(pl.multiple_of(i * blk, blk), blk), :].T
        zero = jnp.zeros_like(qt)
        qq = jnp.concatenate([jnp.where(lo_rows, qt, zero), jnp.where(lo_rows, zero, qt)], axis=1)
        return qq, (ct_ref[0, 0, pl.ds(i, 1), :], ct_ref[0, 1, pl.ds(i, 1), :])

    def scores(qq, ct, c, diagonal):
        r0 = pl.multiple_of(c * blk, blk)
        st = _dot(k_ref[0, 0, pl.ds(r0, blk), :], qq)
        for half in range(2):
            cs = cs_scr[half, pl.ds(r0, blk), :]
            sc = jnp.concatenate(
                [st[:, half * blk + j * LANES:half * blk + (j + 1) * LANES] - cs for j in range(nj)], axis=1)
            if diagonal:
                sc = sc + mask_scr[...]
            s_scr[half, c] = sc
            mx_scr[half] = jnp.maximum(mx_scr[half], jnp.max(sc.reshape(groups, 8, blk), axis=0))

    def weights(c):
        for half in range(2):
            e = jnp.exp2(s_scr[half, c] - mb_scr[half][0:1, :])
            acc_scr[half] = acc_scr[half] + _dot(vt_scr[c, half], e.astype(BF16))

    def finish_scores(ct):
        for half in range(2):
            row_max = jnp.max(mx_scr[half], axis=0, keepdims=True) + ct[half]
            shift = row_max - ct[half]
            mb_scr[half] = jnp.broadcast_to(shift, (8, blk))
        mx_scr[...] = jnp.full_like(mx_scr, NEG)
        acc_scr[...] = jnp.zeros_like(acc_scr)

    def write_out(i):
        ot = jnp.concatenate([acc_scr[half, 0:HEAD_DIM] * (1.0 / acc_scr[half, HEAD_DIM:HEAD_DIM + 1])
                              for half in range(2)], axis=0)
        o_ref[0, 0, pl.ds(pl.multiple_of(i * blk, blk), blk), :] = ot.T.astype(BF16)

    mx_scr[...] = jnp.full_like(mx_scr, NEG)
    qq0, ct0 = load_q(0)
    scores(qq0, ct0, 0, True)
    finish_scores(ct0)

    def tile(i, carry):
        qq, ct = load_q(i + 1)
        n_full = i + 1

        def both(c):
            weights(c)
            scores(qq, ct, c, False)

        def four_chunks(c4, carry1):
            for k in range(4):
                both(4 * c4 + k)
            return carry1

        lax.fori_loop(0, lax.shift_right_logical(n_full, 2), four_chunks, 0)
        done = n_full & ~3

        @pl.when((n_full & 2) != 0)
        def _():
            both(done)
            both(done + 1)

        @pl.when((n_full & 1) != 0)
        def _():
            both(n_full - 1)

        scores(qq, ct, n_full, True)
        write_out(i)
        finish_scores(ct)
        return carry

    lax.fori_loop(0, n_tiles - 1, tile, 0)

    def last(c, carry1):
        weights(c)
        return carry1

    lax.fori_loop(0, n_tiles, last, 0, unroll=4)
    write_out(n_tiles - 1)


def _fox(q, k, v, ct_rows, blk=512):
    b, n_pairs, s, _ = q.shape
    nh = ct_rows.shape[1]
    ct4 = ct_rows.reshape(b, nh, s // blk, blk)
    pair = lambda: pl.BlockSpec((1, 1, s, LANES), lambda bi, p: (bi, p, 0, 0))
    stat = lambda rows: pltpu.VMEM((2, rows, blk), F32)
    return pl.pallas_call(
        functools.partial(_fox_kernel, blk=blk),
        grid=(b, n_pairs),
        in_specs=[pair(), pair(), pair(), pl.BlockSpec((1, 2, s // blk, blk), lambda bi, p: (bi, p, 0, 0))],
        out_specs=pair(),
        out_shape=jax.ShapeDtypeStruct((b, n_pairs, s, LANES), BF16),
        scratch_shapes=[pltpu.VMEM((2, s // blk, blk, blk), F32),
                        pltpu.VMEM((s // blk, 2, HEAD_DIM + V_PAD, blk), BF16),
                        pltpu.VMEM((2, s, LANES), F32), pltpu.VMEM((blk, blk), F32),
                        stat(8), stat(8), stat(HEAD_DIM + V_PAD)],
        compiler_params=_params(("parallel", "parallel"), 48),
        name="fox",
    )(q, k, v, ct4)


def _out_proj_kernel(h_ref, o_ref, w_ref, out_ref):
    for rows in _sub_tiles(h_ref.shape[1]):
        o = jnp.concatenate([o_ref[0, p, rows] for p in range(o_ref.shape[1])], axis=1)
        out_ref[0, rows] = h_ref[0, rows] + _dot(o, w_ref[...])


def _out_proj(h, o, w_o, tm=2 * SUB_TILE):
    b, s, d = h.shape
    n_pairs = o.shape[1]
    return pl.pallas_call(
        _out_proj_kernel,
        grid=(b, s // tm),
        in_specs=[pl.BlockSpec((1, tm, d), lambda bi, i: (bi, i, 0)),
                  pl.BlockSpec((1, n_pairs, tm, LANES), lambda bi, i: (bi, 0, i, 0)), _resident(w_o.shape)],
        out_specs=pl.BlockSpec((1, tm, d), lambda bi, i: (bi, i, 0)),
        out_shape=jax.ShapeDtypeStruct((b, s, d), F32),
        compiler_params=_params(("parallel", "parallel"), 32),
        name="out_proj",
    )(h, o, w_o)


def kernel(x, positions, ffn_norm, ffn_w_in, ffn_w_out, mix_norm, a_w_qkv, a_q_norm, a_k_norm, a_w_o,
           kv_norm, kv_w, kv_b_f, kv_k_norm, b_w_q, b_q_norm, b_w_o):
    b, s, d = x.shape
    depth, n_a = ffn_norm.shape[0], a_w_qkv.shape[0]
    hd = a_w_o.shape[1]
    nh = hd // HEAD_DIM
    a_tile = 256

    head_of = np.arange(MXU_DIM) // HEAD_DIM
    bd = jnp.asarray((head_of[:, None] == head_of[None, :]).astype(np.float32) / HEAD_DIM, BF16)
    inv_freq = ROPE_THETA ** (-jnp.arange(0, ROT_DIM, 2, dtype=F32) / ROT_DIM)
    f_head = jnp.concatenate([inv_freq, inv_freq, jnp.zeros((HEAD_DIM - ROT_DIM,), F32)])
    f_row = jnp.tile(f_head, LANES // HEAD_DIM).reshape(1, LANES)
    tile_gain = lambda gain: jnp.tile(gain, nh).reshape(1, hd)
    expand = jnp.asarray((np.arange(LANES)[:, None] == np.arange(hd)[None, :] // HEAD_DIM).astype(np.float32), BF16)

    ffn = lambda h, layer, j: _ffn(h.reshape(b * s, d), ffn_norm[layer, j], ffn_w_in[layer, j].astype(BF16),
                                   ffn_w_out[layer, j].astype(BF16)).reshape(b, s, d)

    h = x
    k_sh = v_sh = ck_rows = None
    for layer in range(depth):
        if layer == n_a:
            k_sh, v_sh, ck_rows = _shared_kv(h, kv_norm, kv_w, kv_b_f, tile_gain(kv_k_norm), bd)
        h = ffn(h, layer, 0)
        if layer < n_a:
            gq = jnp.stack([tile_gain(a_q_norm[layer, g]) for g in range(len(DILATED_GROUPS))])
            gk = jnp.stack([tile_gain(a_k_norm[layer, g]) for g in range(len(DILATED_GROUPS))])
            qkv = _a_qkv(h, mix_norm[layer], a_w_qkv[layer].astype(BF16), positions, f_row, gq, gk, bd, a_tile)
            attn = [_a_attn(t, hd) for t in qkv]
            h = _a_mix(h, [o for o, _ in attn], [l for _, l in attn], a_w_o[layer].astype(BF16), expand, a_tile)
        else:
            j = layer - n_a
            q = _b_q(h, mix_norm[layer], b_w_q[j].astype(BF16), tile_gain(b_q_norm[j]), bd)
            o = _fox(q, k_sh, v_sh, ck_rows)
            h = _out_proj(h, o, b_w_o[j].astype(BF16))
        h = ffn(h, layer, 1)
    return h
```

```python
import functools

import numpy as np
import jax
import jax.numpy as jnp
from jax import lax
from jax.experimental import pallas as pl
from jax.experimental.pallas import tpu as pltpu

F32 = jnp.float32
BF16 = jnp.bfloat16

HEAD_DIM = 64
DILATED_GROUPS = ((128, 1), (512, 4), (2048, 16))
BAND = 128
ROT_DIM = HEAD_DIM // 4
ROPE_THETA = 500000.0
EPS = 1e-6
LANES = 128
MXU_DIM = 256
NEG = -1e30
LOG2E = 1.4426950408889634
LN2 = 0.6931471805599453
SUB_TILE = 512
V_PAD = 16
MIB = 1024 * 1024


def _params(semantics, vmem_mib):
    return pltpu.CompilerParams(dimension_semantics=semantics, vmem_limit_bytes=vmem_mib * MIB)


def _resident(shape):
    nd = len(shape)
    return pl.BlockSpec(shape, lambda *_: (0,) * nd, pipeline_mode=pl.Buffered(1))


def _dot(a, b):
    return jnp.dot(a, b, preferred_element_type=F32)


def _dot_nt(a, b):
    return lax.dot_general(a, b, (((1,), (1,)), ((), ())), preferred_element_type=F32)


def _rms(x, g):
    ms = jnp.mean(x * x, axis=-1, keepdims=True)
    return x * lax.rsqrt(ms + EPS) * g


def _head_rms(y, bd_ref, gain):
    outs = []
    for c in range(y.shape[1] // MXU_DIM):
        yc = y[:, c * MXU_DIM:(c + 1) * MXU_DIM]
        ms = _dot((yc * yc).astype(BF16), bd_ref[...])
        outs.append(yc * lax.rsqrt(ms + EPS))
    return jnp.concatenate(outs, axis=1) * gain


def _rope_tables(pos_col, f_row):
    ang = pos_col.astype(F32) * f_row
    j = lax.broadcasted_iota(jnp.int32, (1, LANES), 1) % HEAD_DIM
    c, s = jnp.cos(ang), jnp.sin(ang)
    half = ROT_DIM // 2
    s_up = jnp.where(j < half, -s, 0.0)
    s_dn = jnp.where((j >= half) & (j < ROT_DIM), s, 0.0)
    return c, s_up, s_dn


def _rope(y, tabs):
    c, s_up, s_dn = tabs
    half = ROT_DIM // 2
    outs = []
    for k in range(y.shape[1] // LANES):
        yc = y[:, k * LANES:(k + 1) * LANES]
        outs.append(yc * c + pltpu.roll(yc, LANES - half, 1) * s_up + pltpu.roll(yc, half, 1) * s_dn)
    return jnp.concatenate(outs, axis=1)


def _split_bf16(v, parts):
    out, r = [], v
    for _ in range(parts):
        p = r.astype(BF16)
        out.append(p)
        r = r - p.astype(F32)
    return out


def _sub_tiles(n_rows):
    return [slice(r, r + SUB_TILE) for r in range(0, n_rows, SUB_TILE)]


def _ffn_kernel(x_ref, g_ref, win_ref, wout_ref, o_ref, *, d_ff, bounds):
    for rows in _sub_tiles(x_ref.shape[0]):
        x = x_ref[rows]
        xn = _rms(x, g_ref[...]).astype(BF16)
        acc = jnp.zeros_like(x)
        for lo, hi in zip(bounds[:-1], bounds[1:]):
            gate = _dot(xn, win_ref[:, lo:hi])
            up = _dot(xn, win_ref[:, d_ff + lo:d_ff + hi])
            mid = (gate * jax.nn.sigmoid(gate) * up).astype(BF16)
            acc = acc + _dot(mid, wout_ref[lo:hi, :])
        o_ref[rows] = x + 0.5 * acc


def _ffn(h2, g, w_in, w_out, tm=2 * SUB_TILE):
    n, d = h2.shape
    d_ff = w_out.shape[0]
    tiles = pl.cdiv(d_ff, MXU_DIM)
    bounds = (0, min(d_ff, pl.cdiv(tiles, 2) * MXU_DIM), d_ff) if tiles > 1 else (0, d_ff)
    return pl.pallas_call(
        functools.partial(_ffn_kernel, d_ff=d_ff, bounds=bounds),
        grid=(n // tm,),
        in_specs=[pl.BlockSpec((tm, d), lambda i: (i, 0)), _resident((1, d)),
                  _resident(w_in.shape), _resident(w_out.shape)],
        out_specs=pl.BlockSpec((tm, d), lambda i: (i, 0)),
        out_shape=jax.ShapeDtypeStruct((n, d), F32),
        compiler_params=_params(("parallel",), 56),
        name="ffn",
    )(h2, g.reshape(1, d), w_in, w_out)


def _perm_matrix(tm, dil):
    p = np.arange(tm)
    t = (p % (tm // dil)) * dil + p // (tm // dil)
    m = np.zeros((tm, tm), np.float32)
    m[p, t] = 1.0
    return m


def _a_qkv_kernel(x_ref, g_ref, w_ref, p4_ref, p16_ref, pos1_ref, pos4_ref, pos16_ref, f_ref, gq_ref, gk_ref,
                  bd_ref, o1_ref, o4_ref, o16_ref):
    hd = gq_ref.shape[-1]
    xn = _rms(x_ref[0], g_ref[...]).astype(BF16)
    per_group = ((None, pos1_ref, o1_ref), (p4_ref, pos4_ref, o4_ref), (p16_ref, pos16_ref, o16_ref))
    for g, (p_ref, pos_ref, o_ref) in enumerate(per_group):
        xp = xn if p_ref is None else _dot(p_ref[...], xn).astype(BF16)
        tabs = _rope_tables(pos_ref[0], f_ref[...])
        dil, rows = o_ref.shape[1], o_ref.shape[2]
        w0 = g * 3 * hd
        q = _dot(xp, w_ref[:, w0:w0 + hd])
        q = _rope(_head_rms(q, bd_ref, gq_ref[g]), tabs) * (HEAD_DIM ** -0.5 * LOG2E)
        o_ref[0, :, :, 0:hd] = q.astype(BF16).reshape(dil, rows, hd)
        k = _dot(xp, w_ref[:, w0 + hd:w0 + 2 * hd])
        k = _rope(_head_rms(k, bd_ref, gk_ref[g]), tabs)
        o_ref[0, :, :, hd:2 * hd] = k.astype(BF16).reshape(dil, rows, hd)
        v = _dot(xp, w_ref[:, w0 + 2 * hd:w0 + 3 * hd])
        o_ref[0, :, :, 2 * hd:3 * hd] = v.astype(BF16).reshape(dil, rows, hd)


def _a_qkv(h, g, w_qkv, positions, f_row, gq, gk, bd, tm):
    b, s, d = h.shape
    hd = gq.shape[-1]
    dils = [dil for _, dil in DILATED_GROUPS]
    perms = [jnp.asarray(_perm_matrix(tm, dil), BF16) for dil in dils[1:]]
    pos = [positions.reshape(b, s // tm, tm // dil, dil).transpose(0, 1, 3, 2).reshape(b, s, 1) for dil in dils]
    row_spec = lambda last: pl.BlockSpec((1, tm, last), lambda bi, i: (bi, i, 0))
    return pl.pallas_call(
        _a_qkv_kernel,
        grid=(b, s // tm),
        in_specs=[row_spec(d), _resident((1, d)), _resident(w_qkv.shape), _resident((tm, tm)), _resident((tm, tm)),
                  row_spec(1), row_spec(1), row_spec(1), _resident((1, LANES)), _resident(gq.shape),
                  _resident(gk.shape), _resident(bd.shape)],
        out_specs=[pl.BlockSpec((1, dil, tm // dil, 3 * hd), lambda bi, i: (bi, 0, i, 0)) for dil in dils],
        out_shape=[jax.ShapeDtypeStruct((b, dil, s // dil, 3 * hd), BF16) for dil in dils],
        compiler_params=_params(("parallel", "parallel"), 52),
        name="a_qkv",
    )(h, g.reshape(1, d), w_qkv, perms[0], perms[1], pos[0], pos[1], pos[2], f_row, gq, gk, bd)


def _a_attn_kernel(q_ref, kp_ref, kc_ref, vp_ref, vc_ref, o_ref, ml_ref):
    n = pl.program_id(2)
    n_heads = q_ref.shape[-1] // HEAD_DIM
    lane = lax.broadcasted_iota(jnp.int32, (BAND, LANES), 1)
    qi = lax.broadcasted_iota(jnp.int32, (BAND, 2 * BAND), 0)
    kj = lax.broadcasted_iota(jnp.int32, (BAND, 2 * BAND), 1)
    band = (kj >= qi) & (kj <= qi + BAND)
    lo_half = lane < HEAD_DIM
    for res, j in [(res, j) for res in range(q_ref.shape[1]) for j in range(q_ref.shape[2] // BAND)]:
        rows = slice(j * BAND, (j + 1) * BAND)
        before = slice((j - 1) * BAND, j * BAND)
        valid = band & ((n > 0) | (kj >= BAND)) if j == 0 else band
        mask = jnp.where(valid, 0.0, NEG)
        mask = jnp.concatenate([mask, mask], axis=0)
        ml_tile = jnp.zeros((BAND, LANES), F32)
        for p in range(q_ref.shape[-1] // LANES):
            sl = slice(p * LANES, (p + 1) * LANES)
            q2 = q_ref[0, res, rows, sl]
            k_prev = kp_ref[0, res, :, sl] if j == 0 else kc_ref[0, res, before, sl]
            v_prev = vp_ref[0, res, :, sl] if j == 0 else vc_ref[0, res, before, sl]
            kk = jnp.concatenate([k_prev, kc_ref[0, res, rows, sl]], axis=0)
            vv = jnp.concatenate([v_prev, vc_ref[0, res, rows, sl]], axis=0)
            zero = jnp.zeros_like(q2)
            qq = jnp.concatenate([jnp.where(lo_half, q2, zero), jnp.where(lo_half, zero, q2)], axis=0)
            sc = _dot_nt(qq, kk) + mask
            m = jnp.max(sc, axis=-1, keepdims=True)
            e = jnp.exp2(sc - m)
            l = jnp.sum(e, axis=-1, keepdims=True)
            o = _dot(e.astype(BF16), vv)
            o_ref[0, res, rows, sl] = jnp.where(lo_half, o[:BAND], o[BAND:]).astype(BF16)
            for half in range(2):
                hrows = slice(half * BAND, (half + 1) * BAND)
                head = 2 * p + half
                ml_tile = jnp.where(lane == head, m[hrows], jnp.where(lane == n_heads + head, l[hrows], ml_tile))
        ml_ref[0, res, rows, :] = ml_tile


def _a_attn(qkv, hd, blocks_per_step=8):
    b, dil, sd, _ = qkv.shape
    seq_blocks = min(blocks_per_step, sd // BAND)
    res = min(dil, blocks_per_step // seq_blocks)
    rows = BAND * seq_blocks
    cur = lambda col: pl.BlockSpec((1, res, rows, hd), lambda bi, r, n: (bi, r, n, col))
    prev = lambda col: pl.BlockSpec(
        (1, res, BAND, hd), lambda bi, r, n: (bi, r, jnp.maximum(n * seq_blocks - 1, 0), col))
    return pl.pallas_call(
        _a_attn_kernel,
        grid=(b, dil // res, sd // rows),
        in_specs=[cur(0), prev(1), cur(1), prev(2), cur(2)],
        out_specs=[pl.BlockSpec((1, res, rows, hd), lambda bi, r, n: (bi, r, n, 0)),
                   pl.BlockSpec((1, res, rows, LANES), lambda bi, r, n: (bi, r, n, 0))],
        out_shape=[jax.ShapeDtypeStruct((b, dil, sd, hd), BF16), jax.ShapeDtypeStruct((b, dil, sd, LANES), F32)],
        compiler_params=_params(("parallel", "parallel", "arbitrary"), 32),
        name=f"a_attn_d{dil}",
    )(qkv, qkv, qkv, qkv, qkv)


def _a_mix_kernel(h_ref, o1_ref, o4_ref, o16_ref, l1_ref, l4_ref, l16_ref, pt4_ref, pt16_ref, e_ref, wo_ref,
                  out_ref):
    tm, hd = pt4_ref.shape[0], o1_ref.shape[3]
    n_heads = hd // HEAD_DIM
    is_head = lax.broadcasted_iota(jnp.int32, (tm, LANES), 1) < n_heads
    for j in range(h_ref.shape[1] // tm):
        rows = slice(j * tm, (j + 1) * tm)

        def grouped(ref, dil):
            return ref[0, :, j * (tm // dil):(j + 1) * (tm // dil), :].reshape(tm, ref.shape[-1])

        def unperm_stats(pt_ref, l_ref, dil):
            return sum(_dot(pt_ref[...], part) for part in _split_bf16(grouped(l_ref, dil), 3))

        o = [o1_ref[0, 0, rows].astype(F32),
             _dot(pt4_ref[...], grouped(o4_ref, o4_ref.shape[1])),
             _dot(pt16_ref[...], grouped(o16_ref, o16_ref.shape[1]))]
        ml = [l1_ref[0, 0, rows], unperm_stats(pt4_ref, l4_ref, l4_ref.shape[1]),
              unperm_stats(pt16_ref, l16_ref, l16_ref.shape[1])]
        m = ml
        l = [pltpu.roll(v, LANES - n_heads, 1) for v in ml]
        m_max = jnp.maximum(jnp.maximum(m[0], m[1]), m[2])
        w = [jnp.exp2(v - m_max) for v in m]
        den = w[0] * l[0] + w[1] * l[1] + w[2] * l[2]
        mixed = jnp.zeros((tm, hd), F32)
        for wg, og in zip(w, o):
            coef = jnp.where(is_head, wg / den, 0.0).astype(BF16)
            mixed = mixed + _dot(coef, e_ref[...]) * og
        out_ref[0, rows] = h_ref[0, rows] + _dot(mixed.astype(BF16), wo_ref[...])


def _a_mix(h, outs, lses, w_o, expand, tm, tiles_per_step=2):
    b, s, d = h.shape
    hd = w_o.shape[0]
    dils = [dil for _, dil in DILATED_GROUPS]
    pts = [jnp.asarray(_perm_matrix(tm, dil).T, BF16) for dil in dils[1:]]
    rows = tm * tiles_per_step
    perm_spec = lambda dil, last: pl.BlockSpec((1, dil, rows // dil, last), lambda bi, i: (bi, 0, i, 0))
    return pl.pallas_call(
        _a_mix_kernel,
        grid=(b, s // rows),
        in_specs=[pl.BlockSpec((1, rows, d), lambda bi, i: (bi, i, 0))]
        + [perm_spec(dil, hd) for dil in dils] + [perm_spec(dil, LANES) for dil in dils]
        + [_resident((tm, tm)), _resident((tm, tm)), _resident(expand.shape), _resident(w_o.shape)],
        out_specs=pl.BlockSpec((1, rows, d), lambda bi, i: (bi, i, 0)),
        out_shape=jax.ShapeDtypeStruct((b, s, d), F32),
        compiler_params=_params(("parallel", "parallel"), 32),
        name="a_mix",
    )(h, *outs, *lses, pts[0], pts[1], expand, w_o)


def _store_pairs(ref, y, rows=slice(None)):
    for p in range(ref.shape[1]):
        ref[0, p, rows] = y[:, p * LANES:(p + 1) * LANES].astype(ref.dtype)


def _shared_kv_kernel(x_ref, g_ref, wk_ref, wv_ref, wf_ref, bf_ref, gk_ref, bd_ref, tri_ref,
                      k_ref, v_ref, ck_ref, carry_ref):
    @pl.when(pl.program_id(1) == 0)
    def _():
        carry_ref[...] = jnp.zeros_like(carry_ref)

    for rows in _sub_tiles(x_ref.shape[1]):
        xn = _rms(x_ref[0, rows], g_ref[...]).astype(BF16)
        _store_pairs(k_ref, _head_rms(_dot(xn, wk_ref[...]), bd_ref, gk_ref[...]), rows)
        _store_pairs(v_ref, _dot(xn, wv_ref[...]), rows)
        z = _dot(xn, wf_ref[...]) + bf_ref[...]
        log_f = jnp.minimum(z, 0.0) - jnp.log(1.0 + jnp.exp(-jnp.abs(z)))
        cum = carry_ref[...] + sum(_dot(tri_ref[...], part) for part in _split_bf16(log_f, 3))
        carry_ref[...] = cum[-1:, :]
        ck_ref[0, :, rows] = (cum * LOG2E).T[:ck_ref.shape[1], :]


def _shared_kv(h, g, w_kv, b_f, gk, bd, tm=2 * SUB_TILE):
    b, s, d = h.shape
    hd = gk.shape[-1]
    nh, n_pairs = hd // HEAD_DIM, hd // LANES
    wk, wv = w_kv[:, :hd].astype(BF16), w_kv[:, hd:2 * hd].astype(BF16)
    wf = jnp.pad(w_kv[:, 2 * hd:], ((0, 0), (0, LANES - nh))).astype(BF16)
    bf = jnp.pad(b_f, (0, LANES - nh)).reshape(1, LANES)
    tri = jnp.asarray(np.tril(np.ones((SUB_TILE, SUB_TILE), np.float32)), BF16)
    pairs = lambda: pl.BlockSpec((1, n_pairs, tm, LANES), lambda bi, i: (bi, 0, i, 0))
    return pl.pallas_call(
        _shared_kv_kernel,
        grid=(b, s // tm),
        in_specs=[pl.BlockSpec((1, tm, d), lambda bi, i: (bi, i, 0)), _resident((1, d)), _resident(wk.shape),
                  _resident(wv.shape), _resident(wf.shape), _resident(bf.shape), _resident(gk.shape),
                  _resident(bd.shape), _resident(tri.shape)],
        out_specs=[pairs(), pairs(), pl.BlockSpec((1, nh, tm), lambda bi, i: (bi, 0, i))],
        out_shape=[jax.ShapeDtypeStruct((b, n_pairs, s, LANES), BF16), jax.ShapeDtypeStruct((b, n_pairs, s, LANES), BF16),
                   jax.ShapeDtypeStruct((b, nh, s), F32)],
        scratch_shapes=[pltpu.VMEM((1, LANES), F32)],
        compiler_params=_params(("parallel", "arbitrary"), 40),
        name="shared_kv",
    )(h, g.reshape(1, d), wk, wv, wf, bf, gk, bd, tri)


def _b_q_kernel(x_ref, g_ref, w_ref, gq_ref, bd_ref, o_ref):
    for rows in _sub_tiles(x_ref.shape[1]):
        xn = _rms(x_ref[0, rows], g_ref[...]).astype(BF16)
        q = _head_rms(_dot(xn, w_ref[...]), bd_ref, gq_ref[...]) * (HEAD_DIM ** -0.5 * LOG2E)
        _store_pairs(o_ref, q, rows)


def _b_q(h, g, w_q, gq, bd, tm=2 * SUB_TILE):
    b, s, d = h.shape
    n_pairs = w_q.shape[1] // LANES
    return pl.pallas_call(
        _b_q_kernel,
        grid=(b, s // tm),
        in_specs=[pl.BlockSpec((1, tm, d), lambda bi, i: (bi, i, 0)), _resident((1, d)), _resident(w_q.shape),
                  _resident(gq.shape), _resident(bd.shape)],
        out_specs=pl.BlockSpec((1, n_pairs, tm, LANES), lambda bi, i: (bi, 0, i, 0)),
        out_shape=jax.ShapeDtypeStruct((b, n_pairs, s, LANES), BF16),
        compiler_params=_params(("parallel", "parallel"), 32),
        name="b_q",
    )(h, g.reshape(1, d), w_q, gq, bd)


def _fox_kernel(q_ref, k_ref, v_ref, ct_ref, o_ref, s_scr, vt_scr, cs_scr, mask_scr, mx_scr, mb_scr, acc_scr, *, blk):
    n_tiles = q_ref.shape[2] // blk
    nj = blk // LANES
    groups = blk // 8
    lo_rows = lax.broadcasted_iota(jnp.int32, (LANES, blk), 0) < HEAD_DIM
    key = lax.broadcasted_iota(jnp.int32, (blk, blk), 0)
    qry = lax.broadcasted_iota(jnp.int32, (blk, blk), 1)
    mask_scr[...] = jnp.where(key <= qry, 0.0, NEG)
    ones_rows = jnp.where(lax.broadcasted_iota(jnp.int32, (V_PAD, blk), 0) == 0, 1.0, 0.0).astype(BF16)
    for c in range(n_tiles):
        rows = slice(c * blk, (c + 1) * blk)
        vt = v_ref[0, 0, rows, :].T
        for half in range(2):
            vt_scr[c, half, 0:HEAD_DIM] = vt[half * HEAD_DIM:(half + 1) * HEAD_DIM]
            vt_scr[c, half, HEAD_DIM:HEAD_DIM + V_PAD] = ones_rows
            cs_scr[half, rows] = jnp.broadcast_to(ct_ref[0, half, c:c + 1, :], (LANES, blk)).T

    def load_q(i):
        qt = q_ref[0, 0, pl.ds(pl.multiple_of(i * blk, blk), blk), :].T
        zero = jnp.zeros_like(qt)
        qq = jnp.concatenate([jnp.where(lo_rows, qt, zero), jnp.where(lo_rows, zero, qt)], axis=1)
        return qq, (ct_ref[0, 0, pl.ds(i, 1), :], ct_ref[0, 1, pl.ds(i, 1), :])

    def scores(qq, ct, c, diagonal):
        r0 = pl.multiple_of(c * blk, blk)
        st = _dot(k_ref[0, 0, pl.ds(r0, blk), :], qq)
        for half in range(2):
            cs = cs_scr[half, pl.ds(r0, blk), :]
            sc = jnp.concatenate(
                [st[:, half * blk + j * LANES:half * blk + (j + 1) * LANES] - cs for j in range(nj)], axis=1)
            if diagonal:
                sc = sc + mask_scr[...]
            s_scr[half, c] = sc
            mx_scr[half] = jnp.maximum(mx_scr[half], jnp.max(sc.reshape(groups, 8, blk), axis=0))

    def weights(c):
        for half in range(2):
            e = jnp.exp2(s_scr[half, c] - mb_scr[half][0:1, :])
            acc_scr[half] = acc_scr[half] + _dot(vt_scr[c, half], e.astype(BF16))

    def finish_scores(ct):
        for half in range(2):
            row_max = jnp.max(mx_scr[half], axis=0, keepdims=True) + ct[half]
            shift = row_max - ct[half]
            mb_scr[half] = jnp.broadcast_to(shift, (8, blk))
        mx_scr[...] = jnp.full_like(mx_scr, NEG)
        acc_scr[...] = jnp.zeros_like(acc_scr)

    def write_out(i):
        ot = jnp.concatenate([acc_scr[half, 0:HEAD_DIM] * (1.0 / acc_scr[half, HEAD_DIM:HEAD_DIM + 1])
                              for half in range(2)], axis=0)
        o_ref[0, 0, pl.ds(pl.multiple_of(i * blk, blk), blk), :] = ot.T.astype(BF16)

    mx_scr[...] = jnp.full_like(mx_scr, NEG)
    qq0, ct0 = load_q(0)
    scores(qq0, ct0, 0, True)
    finish_scores(ct0)

    def tile(i, carry):
        qq, ct = load_q(i + 1)
        n_full = i + 1

        def both(c):
            weights(c)
            scores(qq, ct, c, False)

        def four_chunks(c4, carry1):
            for k in range(4):
                both(4 * c4 + k)
            return carry1

        lax.fori_loop(0, lax.shift_right_logical(n_full, 2), four_chunks, 0)
        done = n_full & ~3

        @pl.when((n_full & 2) != 0)
        def _():
            both(done)
            both(done + 1)

        @pl.when((n_full & 1) != 0)
        def _():
            both(n_full - 1)

        scores(qq, ct, n_full, True)
        write_out(i)
        finish_scores(ct)
        return carry

    lax.fori_loop(0, n_tiles - 1, tile, 0)

    def last(c, carry1):
        weights(c)
        return carry1

    lax.fori_loop(0, n_tiles, last, 0, unroll=4)
    write_out(n_tiles - 1)


def _fox(q, k, v, ct_rows, blk=512):
    b, n_pairs, s, _ = q.shape
    nh = ct_rows.shape[1]
    ct4 = ct_rows.reshape(b, nh, s // blk, blk)
    pair = lambda: pl.BlockSpec((1, 1, s, LANES), lambda bi, p: (bi, p, 0, 0))
    stat = lambda rows: pltpu.VMEM((2, rows, blk), F32)
    return pl.pallas_call(
        functools.partial(_fox_kernel, blk=blk),
        grid=(b, n_pairs),
        in_specs=[pair(), pair(), pair(), pl.BlockSpec((1, 2, s // blk, blk), lambda bi, p: (bi, p, 0, 0))],
        out_specs=pair(),
        out_shape=jax.ShapeDtypeStruct((b, n_pairs, s, LANES), BF16),
        scratch_shapes=[pltpu.VMEM((2, s // blk, blk, blk), F32),
                        pltpu.VMEM((s // blk, 2, HEAD_DIM + V_PAD, blk), BF16),
                        pltpu.VMEM((2, s, LANES), F32), pltpu.VMEM((blk, blk), F32),
                        stat(8), stat(8), stat(HEAD_DIM + V_PAD)],
        compiler_params=_params(("parallel", "parallel"), 48),
        name="fox",
    )(q, k, v, ct4)


def _out_proj_kernel(h_ref, o_ref, w_ref, out_ref):
    for rows in _sub_tiles(h_ref.shape[1]):
        o = jnp.concatenate([o_ref[0, p, rows] for p in range(o_ref.shape[1])], axis=1)
        out_ref[0, rows] = h_ref[0, rows] + _dot(o, w_ref[...])


def _out_proj(h, o, w_o, tm=2 * SUB_TILE):
    b, s, d = h.shape
    n_pairs = o.shape[1]
    return pl.pallas_call(
        _out_proj_kernel,
        grid=(b, s // tm),
        in_specs=[pl.BlockSpec((1, tm, d), lambda bi, i: (bi, i, 0)),
                  pl.BlockSpec((1, n_pairs, tm, LANES), lambda bi, i: (bi, 0, i, 0)), _resident(w_o.shape)],
        out_specs=pl.BlockSpec((1, tm, d), lambda bi, i: (bi, i, 0)),
        out_shape=jax.ShapeDtypeStruct((b, s, d), F32),
        compiler_params=_params(("parallel", "parallel"), 32),
        name="out_proj",
    )(h, o, w_o)


def kernel(x, positions, ffn_norm, ffn_w_in, ffn_w_out, mix_norm, a_w_qkv, a_q_norm, a_k_norm, a_w_o,
           kv_norm, kv_w, kv_b_f, kv_k_norm, b_w_q, b_q_norm, b_w_o):
    b, s, d = x.shape
    depth, n_a = ffn_norm.shape[0], a_w_qkv.shape[0]
    hd = a_w_o.shape[1]
    nh = hd // HEAD_DIM
    a_tile = 256

    head_of = np.arange(MXU_DIM) // HEAD_DIM
    bd = jnp.asarray((head_of[:, None] == head_of[None, :]).astype(np.float32) / HEAD_DIM, BF16)
    inv_freq = ROPE_THETA ** (-jnp.arange(0, ROT_DIM, 2, dtype=F32) / ROT_DIM)
    f_head = jnp.concatenate([inv_freq, inv_freq, jnp.zeros((HEAD_DIM - ROT_DIM,), F32)])
    f_row = jnp.tile(f_head, LANES // HEAD_DIM).reshape(1, LANES)
    tile_gain = lambda gain: jnp.tile(gain, nh).reshape(1, hd)
    expand = jnp.asarray((np.arange(LANES)[:, None] == np.arange(hd)[None, :] // HEAD_DIM).astype(np.float32), BF16)

    ffn = lambda h, layer, j: _ffn(h.reshape(b * s, d), ffn_norm[layer, j], ffn_w_in[layer, j].astype(BF16),
                                   ffn_w_out[layer, j].astype(BF16)).reshape(b, s, d)

    h = x
    k_sh = v_sh = ck_rows = None
    for layer in range(depth):
        if layer == n_a:
            k_sh, v_sh, ck_rows = _shared_kv(h, kv_norm, kv_w, kv_b_f, tile_gain(kv_k_norm), bd)
        h = ffn(h, layer, 0)
        if layer < n_a:
            gq = jnp.stack([tile_gain(a_q_norm[layer, g]) for g in range(len(DILATED_GROUPS))])
            gk = jnp.stack([tile_gain(a_k_norm[layer, g]) for g in range(len(DILATED_GROUPS))])
            qkv = _a_qkv(h, mix_norm[layer], a_w_qkv[layer].astype(BF16), positions, f_row, gq, gk, bd, a_tile)
            attn = [_a_attn(t, hd) for t in qkv]
            h = _a_mix(h, [o for o, _ in attn], [l for _, l in attn], a_w_o[layer].astype(BF16), expand, a_tile)
        else:
            j = layer - n_a
            q = _b_q(h, mix_norm[layer], b_w_q[j].astype(BF16), tile_gain(b_q_norm[j]), bd)
            o = _fox(q, k_sh, v_sh, ck_rows)
            h = _out_proj(h, o, b_w_o[j].astype(BF16))
        h = ffn(h, layer, 1)
    return h
```

```python
import functools

import numpy as np
import jax
import jax.numpy as jnp
from jax import lax
from jax.experimental import pallas as pl
from jax.experimental.pallas import tpu as pltpu

F32 = jnp.float32
BF16 = jnp.bfloat16

HEAD_DIM = 64
DILATED_GROUPS = ((128, 1), (512, 4), (2048, 16))
BAND = 128
ROT_DIM = HEAD_DIM // 4
ROPE_THETA = 500000.0
EPS = 1e-6
LANES = 128
MXU_DIM = 256
NEG = -1e30
LOG2E = 1.4426950408889634
LN2 = 0.6931471805599453
SUB_TILE = 512
V_PAD = 16
MIB = 1024 * 1024


def _params(semantics, vmem_mib):
    return pltpu.CompilerParams(dimension_semantics=semantics, vmem_limit_bytes=vmem_mib * MIB)


def _resident(shape):
    nd = len(shape)
    return pl.BlockSpec(shape, lambda *_: (0,) * nd, pipeline_mode=pl.Buffered(1))


def _dot(a, b):
    return jnp.dot(a, b, preferred_element_type=F32)


def _dot_nt(a, b):
    return lax.dot_general(a, b, (((1,), (1,)), ((), ())), preferred_element_type=F32)


def _rms(x, g):
    ms = jnp.mean(x * x, axis=-1, keepdims=True)
    return x * lax.rsqrt(ms + EPS) * g


def _head_rms(y, bd_ref, gain):
    outs = []
    for c in range(y.shape[1] // MXU_DIM):
        yc = y[:, c * MXU_DIM:(c + 1) * MXU_DIM]
        ms = _dot((yc * yc).astype(BF16), bd_ref[...])
        outs.append(yc * lax.rsqrt(ms + EPS))
    return jnp.concatenate(outs, axis=1) * gain


def _rope_tables(pos_col, f_row):
    ang = pos_col.astype(F32) * f_row
    j = lax.broadcasted_iota(jnp.int32, (1, LANES), 1) % HEAD_DIM
    c, s = jnp.cos(ang), jnp.sin(ang)
    half = ROT_DIM // 2
    s_up = jnp.where(j < half, -s, 0.0)
    s_dn = jnp.where((j >= half) & (j < ROT_DIM), s, 0.0)
    return c, s_up, s_dn


def _rope(y, tabs):
    c, s_up, s_dn = tabs
    half = ROT_DIM // 2
    outs = []
    for k in range(y.shape[1] // LANES):
        yc = y[:, k * LANES:(k + 1) * LANES]
        outs.append(yc * c + pltpu.roll(yc, LANES - half, 1) * s_up + pltpu.roll(yc, half, 1) * s_dn)
    return jnp.concatenate(outs, axis=1)


def _split_bf16(v, parts):
    out, r = [], v
    for _ in range(parts):
        p = r.astype(BF16)
        out.append(p)
        r = r - p.astype(F32)
    return out


def _sub_tiles(n_rows):
    return [slice(r, r + SUB_TILE) for r in range(0, n_rows, SUB_TILE)]


def _ffn_kernel(x_ref, g_ref, win_ref, wout_ref, o_ref, *, d_ff, bounds):
    for rows in _sub_tiles(x_ref.shape[0]):
        x = x_ref[rows]
        xn = _rms(x, g_ref[...]).astype(BF16)
        acc = jnp.zeros_like(x)
        for lo, hi in zip(bounds[:-1], bounds[1:]):
            gate = _dot(xn, win_ref[:, lo:hi])
            up = _dot(xn, win_ref[:, d_ff + lo:d_ff + hi])
            mid = (gate * jax.nn.sigmoid(gate) * up).astype(BF16)
            acc = acc + _dot(mid, wout_ref[lo:hi, :])
        o_ref[rows] = x + 0.5 * acc


def _ffn(h2, g, w_in, w_out, tm=2 * SUB_TILE):
    n, d = h2.shape
    d_ff = w_out.shape[0]
    tiles = pl.cdiv(d_ff, MXU_DIM)
    bounds = (0, min(d_ff, pl.cdiv(tiles, 2) * MXU_DIM), d_ff) if tiles > 1 else (0, d_ff)
    return pl.pallas_call(
        functools.partial(_ffn_kernel, d_ff=d_ff, bounds=bounds),
        grid=(n // tm,),
        in_specs=[pl.BlockSpec((tm, d), lambda i: (i, 0)), _resident((1, d)),
                  _resident(w_in.shape), _resident(w_out.shape)],
        out_specs=pl.BlockSpec((tm, d), lambda i: (i, 0)),
        out_shape=jax.ShapeDtypeStruct((n, d), F32),
        compiler_params=_params(("parallel",), 56),
        name="ffn",
    )(h2, g.reshape(1, d), w_in, w_out)


def _perm_matrix(tm, dil):
    p = np.arange(tm)
    t = (p % (tm // dil)) * dil + p // (tm // dil)
    m = np.zeros((tm, tm), np.float32)
    m[p, t] = 1.0
    return m


def _a_qkv_kernel(x_ref, g_ref, w_ref, p4_ref, p16_ref, pos1_ref, pos4_ref, pos16_ref, f_ref, gq_ref, gk_ref,
                  bd_ref, o1_ref, o4_ref, o16_ref):
    hd = gq_ref.shape[-1]
    tm = p4_ref.shape[0]
    per_group = ((None, pos1_ref, o1_ref), (p4_ref, pos4_ref, o4_ref), (p16_ref, pos16_ref, o16_ref))
    for j in range(x_ref.shape[1] // tm):
        tile = slice(j * tm, (j + 1) * tm)
        xn = _rms(x_ref[0, tile], g_ref[...]).astype(BF16)
        for g, (p_ref, pos_ref, o_ref) in enumerate(per_group):
            xp = xn if p_ref is None else _dot(p_ref[...], xn).astype(BF16)
            tabs = _rope_tables(pos_ref[0, tile], f_ref[...])
            dil = o_ref.shape[1]
            rows = tm // dil
            out = slice(j * rows, (j + 1) * rows)
            w0 = g * 3 * hd
            q = _dot(xp, w_ref[:, w0:w0 + hd])
            q = _rope(_head_rms(q, bd_ref, gq_ref[g]), tabs) * (HEAD_DIM ** -0.5 * LOG2E)
            o_ref[0, :, out, 0:hd] = q.astype(BF16).reshape(dil, rows, hd)
            k = _dot(xp, w_ref[:, w0 + hd:w0 + 2 * hd])
            k = _rope(_head_rms(k, bd_ref, gk_ref[g]), tabs)
            o_ref[0, :, out, hd:2 * hd] = k.astype(BF16).reshape(dil, rows, hd)
            v = _dot(xp, w_ref[:, w0 + 2 * hd:w0 + 3 * hd])
            o_ref[0, :, out, 2 * hd:3 * hd] = v.astype(BF16).reshape(dil, rows, hd)


def _a_qkv(h, g, w_qkv, positions, f_row, gq, gk, bd, tm, tiles_per_step=2):
    b, s, d = h.shape
    hd = gq.shape[-1]
    dils = [dil for _, dil in DILATED_GROUPS]
    perms = [jnp.asarray(_perm_matrix(tm, dil), BF16) for dil in dils[1:]]
    pos = [positions.reshape(b, s // tm, tm // dil, dil).transpose(0, 1, 3, 2).reshape(b, s, 1) for dil in dils]
    rows = tm * tiles_per_step
    row_spec = lambda last: pl.BlockSpec((1, rows, last), lambda bi, i: (bi, i, 0))
    return pl.pallas_call(
        _a_qkv_kernel,
        grid=(b, s // rows),
        in_specs=[row_spec(d), _resident((1, d)), _resident(w_qkv.shape), _resident((tm, tm)), _resident((tm, tm)),
                  row_spec(1), row_spec(1), row_spec(1), _resident((1, LANES)), _resident(gq.shape),
                  _resident(gk.shape), _resident(bd.shape)],
        out_specs=[pl.BlockSpec((1, dil, rows // dil, 3 * hd), lambda bi, i: (bi, 0, i, 0)) for dil in dils],
        out_shape=[jax.ShapeDtypeStruct((b, dil, s // dil, 3 * hd), BF16) for dil in dils],
        compiler_params=_params(("parallel", "parallel"), 56),
        name="a_qkv",
    )(h, g.reshape(1, d), w_qkv, perms[0], perms[1], pos[0], pos[1], pos[2], f_row, gq, gk, bd)


def _a_attn_kernel(q_ref, kp_ref, kc_ref, vp_ref, vc_ref, o_ref, ml_ref):
    n = pl.program_id(2)
    n_heads = q_ref.shape[-1] // HEAD_DIM
    lane = lax.broadcasted_iota(jnp.int32, (BAND, LANES), 1)
    qi = lax.broadcasted_iota(jnp.int32, (BAND, 2 * BAND), 0)
    kj = lax.broadcasted_iota(jnp.int32, (BAND, 2 * BAND), 1)
    band = (kj >= qi) & (kj <= qi + BAND)
    lo_half = lane < HEAD_DIM
    for res, j in [(res, j) for res in range(q_ref.shape[1]) for j in range(q_ref.shape[2] // BAND)]:
        rows = slice(j * BAND, (j + 1) * BAND)
        before = slice((j - 1) * BAND, j * BAND)
        valid = band & ((n > 0) | (kj >= BAND)) if j == 0 else band
        mask = jnp.where(valid, 0.0, NEG)
        mask = jnp.concatenate([mask, mask], axis=0)
        ml_tile = jnp.zeros((BAND, LANES), F32)
        for p in range(q_ref.shape[-1] // LANES):
            sl = slice(p * LANES, (p + 1) * LANES)
            q2 = q_ref[0, res, rows, sl]
            k_prev = kp_ref[0, res, :, sl] if j == 0 else kc_ref[0, res, before, sl]
            v_prev = vp_ref[0, res, :, sl] if j == 0 else vc_ref[0, res, before, sl]
            kk = jnp.concatenate([k_prev, kc_ref[0, res, rows, sl]], axis=0)
            vv = jnp.concatenate([v_prev, vc_ref[0, res, rows, sl]], axis=0)
            zero = jnp.zeros_like(q2)
            qq = jnp.concatenate([jnp.where(lo_half, q2, zero), jnp.where(lo_half, zero, q2)], axis=0)
            sc = _dot_nt(qq, kk) + mask
            m = jnp.max(sc, axis=-1, keepdims=True)
            e = jnp.exp2(sc - m)
            l = jnp.sum(e, axis=-1, keepdims=True)
            o = _dot(e.astype(BF16), vv)
            o_ref[0, res, rows, sl] = jnp.where(lo_half, o[:BAND], o[BAND:]).astype(BF16)
            for half in range(2):
                hrows = slice(half * BAND, (half + 1) * BAND)
                head = 2 * p + half
                ml_tile = jnp.where(lane == head, m[hrows], jnp.where(lane == n_heads + head, l[hrows], ml_tile))
        ml_ref[0, res, rows, :] = ml_tile


def _a_attn(qkv, hd, blocks_per_step=8):
    b, dil, sd, _ = qkv.shape
    seq_blocks = min(blocks_per_step, sd // BAND)
    res = min(dil, blocks_per_step // seq_blocks)
    rows = BAND * seq_blocks
    cur = lambda col: pl.BlockSpec((1, res, rows, hd), lambda bi, r, n: (bi, r, n, col))
    prev = lambda col: pl.BlockSpec(
        (1, res, BAND, hd), lambda bi, r, n: (bi, r, jnp.maximum(n * seq_blocks - 1, 0), col))
    return pl.pallas_call(
        _a_attn_kernel,
        grid=(b, dil // res, sd // rows),
        in_specs=[cur(0), prev(1), cur(1), prev(2), cur(2)],
        out_specs=[pl.BlockSpec((1, res, rows, hd), lambda bi, r, n: (bi, r, n, 0)),
                   pl.BlockSpec((1, res, rows, LANES), lambda bi, r, n: (bi, r, n, 0))],
        out_shape=[jax.ShapeDtypeStruct((b, dil, sd, hd), BF16), jax.ShapeDtypeStruct((b, dil, sd, LANES), F32)],
        compiler_params=_params(("parallel", "parallel", "arbitrary"), 32),
        name=f"a_attn_d{dil}",
    )(qkv, qkv, qkv, qkv, qkv)


def _a_mix_kernel(h_ref, o1_ref, o4_ref, o16_ref, l1_ref, l4_ref, l16_ref, pt4_ref, pt16_ref, e_ref, wo_ref,
                  out_ref):
    tm, hd = pt4_ref.shape[0], o1_ref.shape[3]
    n_heads = hd // HEAD_DIM
    is_head = lax.broadcasted_iota(jnp.int32, (tm, LANES), 1) < n_heads
    for j in range(h_ref.shape[1] // tm):
        rows = slice(j * tm, (j + 1) * tm)

        def grouped(ref, dil):
            return ref[0, :, j * (tm // dil):(j + 1) * (tm // dil), :].reshape(tm, ref.shape[-1])

        def unperm_stats(pt_ref, l_ref, dil):
            return sum(_dot(pt_ref[...], part) for part in _split_bf16(grouped(l_ref, dil), 3))

        o = [o1_ref[0, 0, rows].astype(F32),
             _dot(pt4_ref[...], grouped(o4_ref, o4_ref.shape[1])),
             _dot(pt16_ref[...], grouped(o16_ref, o16_ref.shape[1]))]
        ml = [l1_ref[0, 0, rows], unperm_stats(pt4_ref, l4_ref, l4_ref.shape[1]),
              unperm_stats(pt16_ref, l16_ref, l16_ref.shape[1])]
        m = ml
        l = [pltpu.roll(v, LANES - n_heads, 1) for v in ml]
        m_max = jnp.maximum(jnp.maximum(m[0], m[1]), m[2])
        w = [jnp.exp2(v - m_max) for v in m]
        den = w[0] * l[0] + w[1] * l[1] + w[2] * l[2]
        mixed = jnp.zeros((tm, hd), F32)
        for wg, og in zip(w, o):
            coef = jnp.where(is_head, wg / den, 0.0).astype(BF16)
            mixed = mixed + _dot(coef, e_ref[...]) * og
        out_ref[0, rows] = h_ref[0, rows] + _dot(mixed.astype(BF16), wo_ref[...])


def _a_mix(h, outs, lses, w_o, expand, tm, tiles_per_step=2):
    b, s, d = h.shape
    hd = w_o.shape[0]
    dils = [dil for _, dil in DILATED_GROUPS]
    pts = [jnp.asarray(_perm_matrix(tm, dil).T, BF16) for dil in dils[1:]]
    rows = tm * tiles_per_step
    perm_spec = lambda dil, last: pl.BlockSpec((1, dil, rows // dil, last), lambda bi, i: (bi, 0, i, 0))
    return pl.pallas_call(
        _a_mix_kernel,
        grid=(b, s // rows),
        in_specs=[pl.BlockSpec((1, rows, d), lambda bi, i: (bi, i, 0))]
        + [perm_spec(dil, hd) for dil in dils] + [perm_spec(dil, LANES) for dil in dils]
        + [_resident((tm, tm)), _resident((tm, tm)), _resident(expand.shape), _resident(w_o.shape)],
        out_specs=pl.BlockSpec((1, rows, d), lambda bi, i: (bi, i, 0)),
        out_shape=jax.ShapeDtypeStruct((b, s, d), F32),
        compiler_params=_params(("parallel", "parallel"), 32),
        name="a_mix",
    )(h, *outs, *lses, pts[0], pts[1], expand, w_o)


def _store_pairs(ref, y, rows=slice(None)):
    for p in range(ref.shape[1]):
        ref[0, p, rows] = y[:, p * LANES:(p + 1) * LANES].astype(ref.dtype)


def _shared_kv_kernel(x_ref, g_ref, wk_ref, wv_ref, wf_ref, bf_ref, gk_ref, bd_ref, tri_ref,
                      k_ref, v_ref, ck_ref, carry_ref):
    @pl.when(pl.program_id(1) == 0)
    def _():
        carry_ref[...] = jnp.zeros_like(carry_ref)

    for rows in _sub_tiles(x_ref.shape[1]):
        xn = _rms(x_ref[0, rows], g_ref[...]).astype(BF16)
        _store_pairs(k_ref, _head_rms(_dot(xn, wk_ref[...]), bd_ref, gk_ref[...]), rows)
        _store_pairs(v_ref, _dot(xn, wv_ref[...]), rows)
        z = _dot(xn, wf_ref[...]) + bf_ref[...]
        log_f = jnp.minimum(z, 0.0) - jnp.log(1.0 + jnp.exp(-jnp.abs(z)))
        cum = carry_ref[...] + sum(_dot(tri_ref[...], part) for part in _split_bf16(log_f, 3))
        carry_ref[...] = cum[-1:, :]
        ck_ref[0, :, rows] = (cum * LOG2E).T[:ck_ref.shape[1], :]


def _shared_kv(h, g, w_kv, b_f, gk, bd, tm=2 * SUB_TILE):
    b, s, d = h.shape
    hd = gk.shape[-1]
    nh, n_pairs = hd // HEAD_DIM, hd // LANES
    wk, wv = w_kv[:, :hd].astype(BF16), w_kv[:, hd:2 * hd].astype(BF16)
    wf = jnp.pad(w_kv[:, 2 * hd:], ((0, 0), (0, LANES - nh))).astype(BF16)
    bf = jnp.pad(b_f, (0, LANES - nh)).reshape(1, LANES)
    tri = jnp.asarray(np.tril(np.ones((SUB_TILE, SUB_TILE), np.float32)), BF16)
    pairs = lambda: pl.BlockSpec((1, n_pairs, tm, LANES), lambda bi, i: (bi, 0, i, 0))
    return pl.pallas_call(
        _shared_kv_kernel,
        grid=(b, s // tm),
        in_specs=[pl.BlockSpec((1, tm, d), lambda bi, i: (bi, i, 0)), _resident((1, d)), _resident(wk.shape),
                  _resident(wv.shape), _resident(wf.shape), _resident(bf.shape), _resident(gk.shape),
                  _resident(bd.shape), _resident(tri.shape)],
        out_specs=[pairs(), pairs(), pl.BlockSpec((1, nh, tm), lambda bi, i: (bi, 0, i))],
        out_shape=[jax.ShapeDtypeStruct((b, n_pairs, s, LANES), BF16), jax.ShapeDtypeStruct((b, n_pairs, s, LANES), BF16),
                   jax.ShapeDtypeStruct((b, nh, s), F32)],
        scratch_shapes=[pltpu.VMEM((1, LANES), F32)],
        compiler_params=_params(("parallel", "arbitrary"), 40),
        name="shared_kv",
    )(h, g.reshape(1, d), wk, wv, wf, bf, gk, bd, tri)


def _b_q_kernel(x_ref, g_ref, w_ref, gq_ref, bd_ref, o_ref):
    for rows in _sub_tiles(x_ref.shape[1]):
        xn = _rms(x_ref[0, rows], g_ref[...]).astype(BF16)
        q = _head_rms(_dot(xn, w_ref[...]), bd_ref, gq_ref[...]) * (HEAD_DIM ** -0.5 * LOG2E)
        _store_pairs(o_ref, q, rows)


def _b_q(h, g, w_q, gq, bd, tm=2 * SUB_TILE):
    b, s, d = h.shape
    n_pairs = w_q.shape[1] // LANES
    return pl.pallas_call(
        _b_q_kernel,
        grid=(b, s // tm),
        in_specs=[pl.BlockSpec((1, tm, d), lambda bi, i: (bi, i, 0)), _resident((1, d)), _resident(w_q.shape),
                  _resident(gq.shape), _resident(bd.shape)],
        out_specs=pl.BlockSpec((1, n_pairs, tm, LANES), lambda bi, i: (bi, 0, i, 0)),
        out_shape=jax.ShapeDtypeStruct((b, n_pairs, s, LANES), BF16),
        compiler_params=_params(("parallel", "parallel"), 32),
        name="b_q",
    )(h, g.reshape(1, d), w_q, gq, bd)


def _fox_kernel(q_ref, k_ref, v_ref, ct_ref, o_ref, s_scr, vt_scr, cs_scr, mask_scr, mx_scr, mb_scr, acc_scr, *, blk):
    n_tiles = q_ref.shape[2] // blk
    nj = blk // LANES
    groups = blk // 8
    lo_rows = lax.broadcasted_iota(jnp.int32, (LANES, blk), 0) < HEAD_DIM
    key = lax.broadcasted_iota(jnp.int32, (blk, blk), 0)
    qry = lax.broadcasted_iota(jnp.int32, (blk, blk), 1)
    mask_scr[...] = jnp.where(key <= qry, 0.0, NEG)
    ones_rows = jnp.where(lax.broadcasted_iota(jnp.int32, (V_PAD, blk), 0) == 0, 1.0, 0.0).astype(BF16)
    for c in range(n_tiles):
        rows = slice(c * blk, (c + 1) * blk)
        vt = v_ref[0, 0, rows, :].T
        for half in range(2):
            vt_scr[c, half, 0:HEAD_DIM] = vt[half * HEAD_DIM:(half + 1) * HEAD_DIM]
            vt_scr[c, half, HEAD_DIM:HEAD_DIM + V_PAD] = ones_rows
            cs_scr[half, rows] = jnp.broadcast_to(ct_ref[0, half, c:c + 1, :], (LANES, blk)).T

    def load_q(i):
        qt = q_ref[0, 0, pl.ds(pl.multiple_of(i * blk, blk), blk), :].T
        zero = jnp.zeros_like(qt)
        qq = jnp.concatenate([jnp.where(lo_rows, qt, zero), jnp.where(lo_rows, zero, qt)], axis=1)
        return qq, (ct_ref[0, 0, pl.ds(i, 1), :], ct_ref[0, 1, pl.ds(i, 1), :])

    def scores(qq, ct, c, diagonal):
        r0 = pl.multiple_of(c * blk, blk)
        st = _dot(k_ref[0, 0, pl.ds(r0, blk), :], qq)
        for half in range(2):
            cs = cs_scr[half, pl.ds(r0, blk), :]
            sc = jnp.concatenate(
                [st[:, half * blk + j * LANES:half * blk + (j + 1) * LANES] - cs for j in range(nj)], axis=1)
            if diagonal:
                sc = sc + mask_scr[...]
            s_scr[half, c] = sc
            mx_scr[half] = jnp.maximum(mx_scr[half], jnp.max(sc.reshape(groups, 8, blk), axis=0))

    def weights(c):
        for half in range(2):
            e = jnp.exp2(s_scr[half, c] - mb_scr[half][0:1, :])
            acc_scr[half] = acc_scr[half] + _dot(vt_scr[c, half], e.astype(BF16))

    def finish_scores(ct):
        for half in range(2):
            row_max = jnp.max(mx_scr[half], axis=0, keepdims=True) + ct[half]
            shift = row_max - ct[half]
            mb_scr[half] = jnp.broadcast_to(shift, (8, blk))
        mx_scr[...] = jnp.full_like(mx_scr, NEG)
        acc_scr[...] = jnp.zeros_like(acc_scr)

    def write_out(i):
        ot = jnp.concatenate([acc_scr[half, 0:HEAD_DIM] * (1.0 / acc_scr[half, HEAD_DIM:HEAD_DIM + 1])
                              for half in range(2)], axis=0)
        o_ref[0, 0, pl.ds(pl.multiple_of(i * blk, blk), blk), :] = ot.T.astype(BF16)

    mx_scr[...] = jnp.full_like(mx_scr, NEG)
    qq0, ct0 = load_q(0)
    scores(qq0, ct0, 0, True)
    finish_scores(ct0)

    def tile(i, carry):
        qq, ct = load_q(i + 1)
        n_full = i + 1

        def both(c):
            weights(c)
            scores(qq, ct, c, False)

        def four_chunks(c4, carry1):
            for k in range(4):
                both(4 * c4 + k)
            return carry1

        lax.fori_loop(0, lax.shift_right_logical(n_full, 2), four_chunks, 0)
        done = n_full & ~3

        @pl.when((n_full & 2) != 0)
        def _():
            both(done)
            both(done + 1)

        @pl.when((n_full & 1) != 0)
        def _():
            both(n_full - 1)

        scores(qq, ct, n_full, True)
        write_out(i)
        finish_scores(ct)
        return carry

    lax.fori_loop(0, n_tiles - 1, tile, 0)

    def last(c, carry1):
        weights(c)
        return carry1

    lax.fori_loop(0, n_tiles, last, 0, unroll=4)
    write_out(n_tiles - 1)


def _fox(q, k, v, ct_rows, blk=512):
    b, n_pairs, s, _ = q.shape
    nh = ct_rows.shape[1]
    ct4 = ct_rows.reshape(b, nh, s // blk, blk)
    pair = lambda: pl.BlockSpec((1, 1, s, LANES), lambda bi, p: (bi, p, 0, 0))
    stat = lambda rows: pltpu.VMEM((2, rows, blk), F32)
    return pl.pallas_call(
        functools.partial(_fox_kernel, blk=blk),
        grid=(b, n_pairs),
        in_specs=[pair(), pair(), pair(), pl.BlockSpec((1, 2, s // blk, blk), lambda bi, p: (bi, p, 0, 0))],
        out_specs=pair(),
        out_shape=jax.ShapeDtypeStruct((b, n_pairs, s, LANES), BF16),
        scratch_shapes=[pltpu.VMEM((2, s // blk, blk, blk), F32),
                        pltpu.VMEM((s // blk, 2, HEAD_DIM + V_PAD, blk), BF16),
                        pltpu.VMEM((2, s, LANES), F32), pltpu.VMEM((blk, blk), F32),
                        stat(8), stat(8), stat(HEAD_DIM + V_PAD)],
        compiler_params=_params(("parallel", "parallel"), 48),
        name="fox",
    )(q, k, v, ct4)


def _out_proj_kernel(h_ref, o_ref, w_ref, out_ref):
    for rows in _sub_tiles(h_ref.shape[1]):
        o = jnp.concatenate([o_ref[0, p, rows] for p in range(o_ref.shape[1])], axis=1)
        out_ref[0, rows] = h_ref[0, rows] + _dot(o, w_ref[...])


def _out_proj(h, o, w_o, tm=2 * SUB_TILE):
    b, s, d = h.shape
    n_pairs = o.shape[1]
    return pl.pallas_call(
        _out_proj_kernel,
        grid=(b, s // tm),
        in_specs=[pl.BlockSpec((1, tm, d), lambda bi, i: (bi, i, 0)),
                  pl.BlockSpec((1, n_pairs, tm, LANES), lambda bi, i: (bi, 0, i, 0)), _resident(w_o.shape)],
        out_specs=pl.BlockSpec((1, tm, d), lambda bi, i: (bi, i, 0)),
        out_shape=jax.ShapeDtypeStruct((b, s, d), F32),
        compiler_params=_params(("parallel", "parallel"), 32),
        name="out_proj",
    )(h, o, w_o)


def kernel(x, positions, ffn_norm, ffn_w_in, ffn_w_out, mix_norm, a_w_qkv, a_q_norm, a_k_norm, a_w_o,
           kv_norm, kv_w, kv_b_f, kv_k_norm, b_w_q, b_q_norm, b_w_o):
    b, s, d = x.shape
    depth, n_a = ffn_norm.shape[0], a_w_qkv.shape[0]
    hd = a_w_o.shape[1]
    nh = hd // HEAD_DIM
    a_tile = 256

    head_of = np.arange(MXU_DIM) // HEAD_DIM
    bd = jnp.asarray((head_of[:, None] == head_of[None, :]).astype(np.float32) / HEAD_DIM, BF16)
    inv_freq = ROPE_THETA ** (-jnp.arange(0, ROT_DIM, 2, dtype=F32) / ROT_DIM)
    f_head = jnp.concatenate([inv_freq, inv_freq, jnp.zeros((HEAD_DIM - ROT_DIM,), F32)])
    f_row = jnp.tile(f_head, LANES // HEAD_DIM).reshape(1, LANES)
    tile_gain = lambda gain: jnp.tile(gain, nh).reshape(1, hd)
    expand = jnp.asarray((np.arange(LANES)[:, None] == np.arange(hd)[None, :] // HEAD_DIM).astype(np.float32), BF16)

    ffn = lambda h, layer, j: _ffn(h.reshape(b * s, d), ffn_norm[layer, j], ffn_w_in[layer, j].astype(BF16),
                                   ffn_w_out[layer, j].astype(BF16)).reshape(b, s, d)

    h = x
    k_sh = v_sh = ck_rows = None
    for layer in range(depth):
        if layer == n_a:
            k_sh, v_sh, ck_rows = _shared_kv(h, kv_norm, kv_w, kv_b_f, tile_gain(kv_k_norm), bd)
        h = ffn(h, layer, 0)
        if layer < n_a:
            gq = jnp.stack([tile_gain(a_q_norm[layer, g]) for g in range(len(DILATED_GROUPS))])
            gk = jnp.stack([tile_gain(a_k_norm[layer, g]) for g in range(len(DILATED_GROUPS))])
            qkv = _a_qkv(h, mix_norm[layer], a_w_qkv[layer].astype(BF16), positions, f_row, gq, gk, bd, a_tile)
            attn = [_a_attn(t, hd) for t in qkv]
            h = _a_mix(h, [o for o, _ in attn], [l for _, l in attn], a_w_o[layer].astype(BF16), expand, a_tile)
        else:
            j = layer - n_a
            q = _b_q(h, mix_norm[layer], b_w_q[j].astype(BF16), tile_gain(b_q_norm[j]), bd)
            o = _fox(q, k_sh, v_sh, ck_rows)
            h = _out_proj(h, o, b_w_o[j].astype(BF16))
        h = ffn(h, layer, 1)
    return h
```
